```python
import jax, jax.numpy as jnp
from jax import lax
import numpy as np

D_MODEL = 1024
BATCH = 8
SEQ = 2048
DEPTH = 2
DEC_BATCH = 128
DEC_SEQ = 4
PAST_LEN = 8192
PAGE_SIZE = 128

N_MIXERS = 2
N_ATTN_LAYERS = (DEPTH + 1) // 2
N_SGU_LAYERS = DEPTH // 2
N_HEADS = 8
QK_NOPE = 128
QK_ROPE = 64
V_HEAD = 128
Q_LORA = 384
KV_LORA = 256
ROPE_THETA = 10000.0
SOFTMAX_SCALE = (QK_NOPE + QK_ROPE) ** -0.5
QUERY_BLOCK = 128
SGU_CHUNK = 128
SGU_HALF = 3 * D_MODEL
SGU_GROUPS = 8
SGU_GROUP_DIM = SGU_HALF // SGU_GROUPS
N_EXPERT_GROUPS = 4
EXPERTS_PER_GROUP = 8
N_EXPERTS = N_EXPERT_GROUPS * EXPERTS_PER_GROUP
TOP_K_EXPERTS = 2
EXPERT_HIDDEN = 256
PLE_DIM = 256
RMS_EPS = 1e-6
LN_EPS = 1e-5
NEG_INF = -1e30

kernel_name = 'hybrid_mla_chunkgmlp_hmoe_step'


def _rmsnorm(x, g):
    xf = x.astype(jnp.float32)
    y = xf * lax.rsqrt(jnp.mean(xf * xf, axis=-1, keepdims=True) + RMS_EPS)
    return (y * g.astype(jnp.float32)).astype(x.dtype)


def _layernorm(x, g, b):
    xf = x.astype(jnp.float32)
    mu = jnp.mean(xf, axis=-1, keepdims=True)
    var = jnp.mean(jnp.square(xf - mu), axis=-1, keepdims=True)
    y = (xf - mu) * lax.rsqrt(var + LN_EPS)
    return (y * g.astype(jnp.float32) + b.astype(jnp.float32)).astype(x.dtype)


def _rope(x, pos):
    half = QK_ROPE // 2
    inv_freq = 1.0 / (ROPE_THETA ** (jnp.arange(0, QK_ROPE, 2, dtype=jnp.float32) / QK_ROPE))
    ang = pos.astype(jnp.float32)[:, None] * inv_freq[None, :]
    bshape = (pos.shape[0],) + (1,) * (x.ndim - 3) + (half,)
    cos = jnp.cos(ang).reshape(bshape)
    sin = jnp.sin(ang).reshape(bshape)
    xf = x.astype(jnp.float32)
    x1, x2 = xf[..., :half], xf[..., half:]
    return jnp.concatenate([x1 * cos - x2 * sin, x2 * cos + x1 * sin], axis=-1).astype(x.dtype)


def _mla_project(hn, pos, w_in, g_q, g_kv, w_qb, w_uk):
    a = hn @ w_in
    cq = _rmsnorm(a[..., :Q_LORA], g_q)
    ckv = _rmsnorm(a[..., Q_LORA:Q_LORA + KV_LORA], g_kv)
    krope = _rope(a[..., Q_LORA + KV_LORA:], pos)
    q = (cq @ w_qb).reshape(cq.shape[:-1] + (N_HEADS, QK_NOPE + QK_ROPE))
    q_lat = jnp.einsum('bshn,chn->bshc', q[..., :QK_NOPE], w_uk)
    q_rope = _rope(q[..., QK_NOPE:], pos)
    return q_lat, q_rope, ckv, krope


def _mla_output(o_lat, w_uv, w_o):
    o = jnp.einsum('bshc,chv->bshv', o_lat, w_uv)
    return o.reshape(o.shape[:2] + (N_HEADS * V_HEAD,)) @ w_o


def _mla_prompt_attend(q_lat, q_rope, ckv, krope):
    b, s, h, c = q_lat.shape
    nb = s // QUERY_BLOCK
    ql = q_lat.reshape(b, nb, QUERY_BLOCK, h, c).transpose(1, 0, 2, 3, 4)
    qr = q_rope.reshape(b, nb, QUERY_BLOCK, h, QK_ROPE).transpose(1, 0, 2, 3, 4)
    qpos = jnp.arange(s, dtype=jnp.int32).reshape(nb, QUERY_BLOCK)
    kpos = jnp.arange(s, dtype=jnp.int32)

    def one_block(args):
        qlb, qrb, qp = args
        sc = (jnp.einsum('bqhc,bkc->bhqk', qlb, ckv)
              + jnp.einsum('bqhr,bkr->bhqk', qrb, krope)).astype(jnp.float32) * SOFTMAX_SCALE
        sc = jnp.where(kpos[None, None, None, :] <= qp[None, None, :, None], sc, NEG_INF)
        pr = jax.nn.softmax(sc, axis=-1).astype(ckv.dtype)
        return jnp.einsum('bhqk,bkc->bqhc', pr, ckv)

    out = lax.map(one_block, (ql, qr, qpos))
    return out.transpose(1, 0, 2, 3, 4).reshape(b, s, h, c)


def _mla_sample_attend(q_lat, q_rope, ckv_new, krope_new, ckv_past, krope_past):
    n_new = q_lat.shape[1]
    n_past = ckv_past.shape[1]
    s_past = (jnp.einsum('bqhc,bkc->bhqk', q_lat, ckv_past)
              + jnp.einsum('bqhr,bkr->bhqk', q_rope, krope_past)).astype(jnp.float32) * SOFTMAX_SCALE
    s_new = (jnp.einsum('bqhc,bkc->bhqk', q_lat, ckv_new)
             + jnp.einsum('bqhr,bkr->bhqk', q_rope, krope_new)).astype(jnp.float32) * SOFTMAX_SCALE
    causal = jnp.tril(jnp.ones((n_new, n_new), dtype=bool))
    s_new = jnp.where(causal[None, None], s_new, NEG_INF)
    pr = jax.nn.softmax(jnp.concatenate([s_past, s_new], axis=-1), axis=-1).astype(ckv_new.dtype)
    return (jnp.einsum('bhqk,bkc->bqhc', pr[..., :n_past], ckv_past)
            + jnp.einsum('bhqk,bkc->bqhc', pr[..., n_past:], ckv_new))


def _sgu_project(hn, w_in, b_in, g_v, b_v):
    z = jax.nn.gelu(hn @ w_in + b_in)
    u, v = jnp.split(z, 2, axis=-1)
    return u, _layernorm(v, g_v, b_v)


def _spatial_gate(v, w_s, b_s):
    b, n, l, _ = v.shape
    vg = v.reshape(b, n, l, SGU_GROUPS, SGU_GROUP_DIM)
    tril = jnp.tril(jnp.ones((SGU_CHUNK, SGU_CHUNK), dtype=bool))
    wm = jnp.where(tril[None], w_s, 0.0).astype(v.dtype)[:, :l, :l]
    s = jnp.einsum('grt,bntgc->bnrgc', wm, vg) + b_s[:, :l].T[:, :, None]
    return s.reshape(b, n, l, SGU_HALF)


def _hier_moe(hn, w_rg, b_rg, w_re, b_re, w_gate, w_up, w_down):
    lead = hn.shape[:-1]
    t = hn.reshape(-1, D_MODEL)
    n_tok = t.shape[0]
    pg = jax.nn.softmax((t @ w_rg + b_rg).astype(jnp.float32), axis=-1)
    pg_top, g_idx = lax.top_k(pg, 1)
    le = (t @ w_re + b_re).astype(jnp.float32).reshape(n_tok, N_EXPERT_GROUPS, EXPERTS_PER_GROUP)
    le_g = jnp.take_along_axis(le, g_idx[:, :, None], axis=1)[:, 0]
    pe = jax.nn.softmax(le_g, axis=-1)
    pe_top, e_idx = lax.top_k(pe, TOP_K_EXPERTS)
    w = pg_top * pe_top / jnp.sum(pe_top, axis=-1, keepdims=True)
    eid = g_idx * EXPERTS_PER_GROUP + e_idx
    gates = jnp.sum(jax.nn.one_hot(eid, N_EXPERTS, dtype=jnp.float32) * w[..., None], axis=1)
    h1 = jnp.einsum('td,edf->tef', t, w_gate)
    h2 = jnp.einsum('td,edf->tef', t, w_up)
    a = jax.nn.silu(h1) * h2 * gates[:, :, None].astype(t.dtype)
    y = jnp.einsum('tef,efd->td', a, w_down)
    return y.reshape(lead + (D_MODEL,))


def _ple(h, p_i, g, w_proj, w_gate):
    return jax.nn.sigmoid(_rmsnorm(h, g) @ w_gate) * (p_i @ w_proj)


def setup_inputs(seed: int = 0) -> dict:
    key = jax.random.key(seed)
    ks = iter(jax.random.split(key, 48))
    nrm = lambda shape, scale: jax.random.normal(next(ks), shape, jnp.float32) * scale
    gain = lambda shape: 1.0 + 0.02 * jax.random.normal(next(ks), shape, jnp.float32)
    n_pages = PAST_LEN // PAGE_SIZE
    n_used = DEC_BATCH * n_pages
    n_phys = n_used + max(1, n_used // 4)
    d = D_MODEL
    inp = {}
    inp['x_prompt'] = nrm((BATCH, SEQ, d), 1.0)
    inp['x_sample'] = nrm((DEC_BATCH, DEC_SEQ, d), 1.0)
    inp['cache_ckv'] = nrm((N_ATTN_LAYERS, n_phys, PAGE_SIZE, KV_LORA), 1.0)
    inp['cache_krope'] = nrm((N_ATTN_LAYERS, n_phys, PAGE_SIZE, QK_ROPE), 1.0)
    inp['page_table'] = jax.random.permutation(next(ks), n_phys)[:n_used].reshape(DEC_BATCH, n_pages).astype(jnp.int32)
    inp['p_prompt'] = nrm((DEPTH, BATCH, SEQ, PLE_DIM), 1.0)
    inp['p_sample'] = nrm((DEPTH, DEC_BATCH, DEC_SEQ, PLE_DIM), 1.0)
    inp['g_mix'] = gain((DEPTH, d))
    inp['g_ffn'] = gain((DEPTH, d))
    inp['g_ple'] = gain((DEPTH, d))
    inp['g_final'] = gain((d,))
    inp['mla_w_in'] = nrm((N_ATTN_LAYERS, d, Q_LORA + KV_LORA + QK_ROPE), d ** -0.5)
    inp['mla_g_q'] = gain((N_ATTN_LAYERS, Q_LORA))
    inp['mla_g_kv'] = gain((N_ATTN_LAYERS, KV_LORA))
    inp['mla_w_qb'] = nrm((N_ATTN_LAYERS, Q_LORA, N_HEADS * (QK_NOPE + QK_ROPE)), Q_LORA ** -0.5)
    inp['mla_w_uk'] = nrm((N_ATTN_LAYERS, KV_LORA, N_HEADS, QK_NOPE), KV_LORA ** -0.5)
    inp['mla_w_uv'] = nrm((N_ATTN_LAYERS, KV_LORA, N_HEADS, V_HEAD), KV_LORA ** -0.5)
    inp['mla_w_o'] = nrm((N_ATTN_LAYERS, N_HEADS * V_HEAD, d), (N_HEADS * V_HEAD) ** -0.5)
    inp['sgu_w_in'] = nrm((N_SGU_LAYERS, d, 2 * SGU_HALF), d ** -0.5)
    inp['sgu_b_in'] = nrm((N_SGU_LAYERS, 2 * SGU_HALF), 0.02)
    inp['sgu_g_v'] = gain((N_SGU_LAYERS, SGU_HALF))
    inp['sgu_b_v'] = nrm((N_SGU_LAYERS, SGU_HALF), 0.02)
    inp['sgu_w_s'] = nrm((N_SGU_LAYERS, SGU_GROUPS, SGU_CHUNK, SGU_CHUNK), SGU_CHUNK ** -0.5)
    inp['sgu_b_s'] = gain((N_SGU_LAYERS, SGU_GROUPS, SGU_CHUNK))
    inp['sgu_w_out'] = nrm((N_SGU_LAYERS, SGU_HALF, d), SGU_HALF ** -0.5)
    inp['moe_w_rg'] = nrm((DEPTH, d, N_EXPERT_GROUPS), d ** -0.5)
    inp['moe_b_rg'] = nrm((DEPTH, N_EXPERT_GROUPS), 0.01)
    inp['moe_w_re'] = nrm((DEPTH, d, N_EXPERTS), d ** -0.5)
    inp['moe_b_re'] = nrm((DEPTH, N_EXPERTS), 0.01)
    inp['moe_w_gate'] = nrm((DEPTH, N_EXPERTS, d, EXPERT_HIDDEN), d ** -0.5)
    inp['moe_w_up'] = nrm((DEPTH, N_EXPERTS, d, EXPERT_HIDDEN), d ** -0.5)
    inp['moe_w_down'] = nrm((DEPTH, N_EXPERTS, EXPERT_HIDDEN, d), EXPERT_HIDDEN ** -0.5)
    inp['ple_w_proj'] = nrm((DEPTH, PLE_DIM, d), PLE_DIM ** -0.5)
    inp['ple_w_gate'] = nrm((DEPTH, d, d), d ** -0.5)
    return inp


def reference(x_prompt, x_sample, cache_ckv, cache_krope, page_table, p_prompt, p_sample,
              g_mix, g_ffn, g_ple, g_final,
              mla_w_in, mla_g_q, mla_g_kv, mla_w_qb, mla_w_uk, mla_w_uv, mla_w_o,
              sgu_w_in, sgu_b_in, sgu_g_v, sgu_b_v, sgu_w_s, sgu_b_s, sgu_w_out,
              moe_w_rg, moe_b_rg, moe_w_re, moe_b_re, moe_w_gate, moe_w_up, moe_w_down,
              ple_w_proj, ple_w_gate):
    b_p, n_prompt = x_prompt.shape[0], x_prompt.shape[1]
    b_s, n_new = x_sample.shape[0], x_sample.shape[1]
    pos_p = jnp.arange(n_prompt, dtype=jnp.int32)
    pos_s = PAST_LEN + jnp.arange(n_new, dtype=jnp.int32)
    hp, hs = x_prompt, x_sample
    ckv_p_l, kr_p_l, ckv_s_l, kr_s_l, v_s_l = [], [], [], [], []
    for i in range(DEPTH):
        j = i // N_MIXERS
        hn_p = _rmsnorm(hp, g_mix[i])
        hn_s = _rmsnorm(hs, g_mix[i])
        if i % N_MIXERS == 0:
            wts = (mla_w_in[j], mla_g_q[j], mla_g_kv[j], mla_w_qb[j], mla_w_uk[j])
            ql, qr, c_p, r_p = _mla_project(hn_p, pos_p, *wts)
            o_p = _mla_prompt_attend(ql, qr, c_p, r_p)
            ql_s, qr_s, c_s, r_s = _mla_project(hn_s, pos_s, *wts)
            past_c = cache_ckv[j][page_table].reshape(b_s, -1, KV_LORA)
            past_r = cache_krope[j][page_table].reshape(b_s, -1, QK_ROPE)
            o_s = _mla_sample_attend(ql_s, qr_s, c_s, r_s, past_c, past_r)
            mix_p = _mla_output(o_p, mla_w_uv[j], mla_w_o[j])
            mix_s = _mla_output(o_s, mla_w_uv[j], mla_w_o[j])
            ckv_p_l.append(c_p)
            kr_p_l.append(r_p)
            ckv_s_l.append(c_s)
            kr_s_l.append(r_s)
        else:
            u_p, v_p = _sgu_project(hn_p, sgu_w_in[j], sgu_b_in[j], sgu_g_v[j], sgu_b_v[j])
            s_p = _spatial_gate(v_p.reshape(b_p, n_prompt // SGU_CHUNK, SGU_CHUNK, SGU_HALF),
                                sgu_w_s[j], sgu_b_s[j]).reshape(v_p.shape)
            mix_p = (u_p * s_p) @ sgu_w_out[j]
            u_s, v_s = _sgu_project(hn_s, sgu_w_in[j], sgu_b_in[j], sgu_g_v[j], sgu_b_v[j])
            s_s = _spatial_gate(v_s[:, None], sgu_w_s[j], sgu_b_s[j])[:, 0]
            mix_s = (u_s * s_s) @ sgu_w_out[j]
            v_s_l.append(v_s)
        hp = hp + mix_p
        hs = hs + mix_s
        moe_w = (moe_w_rg[i], moe_b_rg[i], moe_w_re[i], moe_b_re[i], moe_w_gate[i], moe_w_up[i], moe_w_down[i])
        hp = hp + _hier_moe(_rmsnorm(hp, g_ffn[i]), *moe_w)
        hs = hs + _hier_moe(_rmsnorm(hs, g_ffn[i]), *moe_w)
        hp = hp + _ple(hp, p_prompt[i], g_ple[i], ple_w_proj[i], ple_w_gate[i])
        hs = hs + _ple(hs, p_sample[i], g_ple[i], ple_w_proj[i], ple_w_gate[i])
    y_prompt = _rmsnorm(hp, g_final)
    y_sample = _rmsnorm(hs, g_final)
    ckv_prompt = jnp.stack(ckv_p_l)
    krope_prompt = jnp.stack(kr_p_l)
    ckv_sample = jnp.stack(ckv_s_l)
    krope_sample = jnp.stack(kr_s_l)
    sgu_v_sample = jnp.stack(v_s_l)
    return (y_prompt, y_sample, ckv_prompt, krope_prompt, ckv_sample, krope_sample, sgu_v_sample)
```

```python
import functools

import jax
import jax.numpy as jnp
from jax import lax
from jax.experimental import pallas as pl
from jax.experimental.pallas import tpu as pltpu

F32 = jnp.float32
BF16 = jnp.bfloat16

PAST_LEN = 8192
QK_NOPE = 128
QK_ROPE = 64
ROPE_THETA = 10000.0
SOFTMAX_SCALE = (QK_NOPE + QK_ROPE) ** -0.5
SGU_CHUNK = 128
SGU_GROUPS = 8
N_EXPERT_GROUPS = 4
EXPERTS_PER_GROUP = 8
RMS_EPS = 1e-6
LN_EPS = 1e-5
NEG_INF = -1e30

LANES = 128
TOKEN_TILE = 512
SGU_TILE = 256
ATTN_Q_BLOCK = 128
ATTN_KV_CHUNK = 256
QK_WIDTH = 384
ROUTE_GIDX_LANE = 64
ROUTE_RANK_LANE = 65
VMEM_LIMIT = 56 * 1024 * 1024


def _rms(x, g):
    return x * lax.rsqrt(jnp.mean(x * x, axis=-1, keepdims=True) + RMS_EPS) * g


def _dot(a, b):
    return jnp.dot(a, b, preferred_element_type=F32)


def _dot_nt(a, b):
    return lax.dot_general(a, b, (((1,), (1,)), ((), ())), preferred_element_type=F32)


def _params(*sem):
    return pltpu.CompilerParams(dimension_semantics=sem, vmem_limit_bytes=VMEM_LIMIT)


def _mla_proj_kernel(h_ref, tab_ref, gmix_ref, win_ref, gq_ref, gkv_ref, wqn_ref, wqr_ref, wuk_ref,
                     q_ref, k_ref, ckvp_ref, krp_ref, ckvs_ref, krs_ref, *, n_prompt_tiles, n_heads):
    i = pl.program_id(0)
    tm = h_ref.shape[0]
    hn = _rms(h_ref[...], gmix_ref[...]).astype(BF16)
    a = _dot(hn, win_ref[...])
    cq = _rms(a[:, :384], gq_ref[...]).astype(BF16)
    ckv = _rms(a[:, 384:640], gkv_ref[...])
    tab = tab_ref[...]
    low = lax.broadcasted_iota(jnp.int32, (tm, LANES), 1) < QK_ROPE

    def rope(r):
        p = r * tab
        return jnp.where(low, p + pltpu.roll(p, QK_ROPE, axis=1), 0.0)

    kr = rope(a[:, 640:768])
    k_ref[:, :256] = ckv.astype(BF16)
    k_ref[:, 256:] = kr.astype(BF16)

    @pl.when(i < n_prompt_tiles)
    def _():
        ckvp_ref[...] = ckv
        krp_ref[...] = kr[:, :QK_ROPE]

    @pl.when(i >= n_prompt_tiles)
    def _():
        ckvs_ref[...] = ckv
        krs_ref[...] = kr[:, :QK_ROPE]

    qn = _dot(cq, wqn_ref[...]).astype(BF16)
    qr = _dot(cq, wqr_ref[...])
    for hd in range(n_heads):
        sl = slice(hd * LANES, (hd + 1) * LANES)
        q_ref[hd, :, :256] = (_dot(qn[:, sl], wuk_ref[hd]) * SOFTMAX_SCALE).astype(BF16)
        q_ref[hd, :, 256:] = (rope(qr[:, sl]) * SOFTMAX_SCALE).astype(BF16)


def _mla_proj(h, tab, g_mix, w_in, g_q, g_kv, w_qn, w_qr, w_ukT, *, tp, ts):
    t, d = h.shape
    tm = TOKEN_TILE
    nh = w_ukT.shape[0]
    npt = tp // tm
    const = lambda *shape: pl.BlockSpec(shape, lambda i: (0,) * len(shape))
    row = lambda w: pl.BlockSpec((tm, w), lambda i: (i, 0))
    prow = lambda w: pl.BlockSpec((tm, w), lambda i: (jnp.minimum(i, npt - 1), 0))
    srow = lambda w: pl.BlockSpec((tm, w), lambda i: (jnp.maximum(i - npt, 0), 0))
    return pl.pallas_call(
        functools.partial(_mla_proj_kernel, n_prompt_tiles=npt, n_heads=nh),
        grid=(t // tm,),
        in_specs=[row(d), row(LANES), const(1, d), const(*w_in.shape), const(1, 384), const(1, 256),
                  const(*w_qn.shape), const(*w_qr.shape), const(*w_ukT.shape)],
        out_specs=[pl.BlockSpec((nh, tm, QK_WIDTH), lambda i: (0, i, 0)), row(QK_WIDTH),
                   prow(256), prow(QK_ROPE), srow(256), srow(QK_ROPE)],
        out_shape=[jax.ShapeDtypeStruct((nh, t, QK_WIDTH), BF16), jax.ShapeDtypeStruct((t, QK_WIDTH), BF16),
                   jax.ShapeDtypeStruct((tp, 256), F32), jax.ShapeDtypeStruct((tp, QK_ROPE), F32),
                   jax.ShapeDtypeStruct((ts, 256), F32), jax.ShapeDtypeStruct((ts, QK_ROPE), F32)],
        compiler_params=_params("arbitrary"),
        name="mla_proj",
    )(h, tab, g_mix, w_in, g_q, g_kv, w_qn, w_qr, w_ukT)


def _prompt_attn_kernel(q_ref, k_ref, o_ref, m_ref, l_ref, acc_ref):
    j = pl.program_id(1)
    nh, qb, w = q_ref.shape
    rows = nh * qb
    ck = ATTN_KV_CHUNK
    q = q_ref[...].reshape(rows, w)
    m_ref[...] = jnp.full(m_ref.shape, NEG_INF, F32)
    l_ref[...] = jnp.zeros(l_ref.shape, F32)
    acc_ref[...] = jnp.zeros(acc_ref.shape, F32)

    def step(c, masked):
        kc = k_ref[pl.ds(pl.multiple_of(c * ck, ck), ck), :]
        s = _dot_nt(q, kc)
        if masked:
            qpos = j * qb + (lax.broadcasted_iota(jnp.int32, s.shape, 0) & (qb - 1))
            kpos = c * ck + lax.broadcasted_iota(jnp.int32, s.shape, 1)
            s = jnp.where(kpos <= qpos, s, NEG_INF)
        m_prev = m_ref[...]
        m_new = jnp.maximum(m_prev, jnp.max(s, axis=-1, keepdims=True))
        alpha = jnp.exp(m_prev - m_new)
        p = jnp.exp(s - m_new)
        l_ref[...] = alpha * l_ref[...] + jnp.sum(p, axis=-1, keepdims=True)
        acc_ref[...] = alpha * acc_ref[...] + _dot(p.astype(BF16), kc[:, :256])
        m_ref[...] = m_new

    n_full = (j * qb) // ck

    def body(c, carry):
        step(c, False)
        return carry

    lax.fori_loop(0, n_full, body, 0)
    step(n_full, True)
    o = acc_ref[...] / l_ref[...]
    o_ref[...] = o.reshape(nh, qb, 256).astype(BF16)


def _prompt_attn(q, k, *, batch, seq):
    nh = q.shape[0]
    qb = ATTN_Q_BLOCK
    nj = seq // qb
    rows = nh * qb
    return pl.pallas_call(
        _prompt_attn_kernel,
        grid=(batch, nj),
        in_specs=[pl.BlockSpec((nh, qb, QK_WIDTH), lambda b, j: (0, b * nj + j, 0)),
                  pl.BlockSpec((seq, QK_WIDTH), lambda b, j: (b, 0))],
        out_specs=pl.BlockSpec((nh, qb, 256), lambda b, j: (0, b * nj + j, 0)),
        out_shape=jax.ShapeDtypeStruct((nh, batch * seq, 256), BF16),
        scratch_shapes=[pltpu.VMEM((rows, 1), F32), pltpu.VMEM((rows, 1), F32), pltpu.VMEM((rows, 256), F32)],
        compiler_params=_params("arbitrary", "arbitrary"),
        name="prompt_attn",
    )(q, k)


def _sample_attn_kernel(pt_ref, q_ref, knew_ref, ckv_hbm, kr_hbm, o_ref, cbuf, rbuf, kb, s_ref, sem,
                        *, n_pages, n_new):
    b = pl.program_id(0)
    nb = pl.num_programs(0)
    slot = b % 2
    half = cbuf.shape[2] // 2
    prow = ckv_hbm.shape[1]

    def page_copies(bb, sl, p):
        page = pt_ref[bb, p]
        dst = pl.ds(p * prow, prow)
        return (pltpu.make_async_copy(ckv_hbm.at[page], cbuf.at[sl, dst], sem.at[0, sl]),
                pltpu.make_async_copy(kr_hbm.at[page], rbuf.at[sl, dst], sem.at[1, sl]))

    def issue(bb, sl):
        def body(p, c):
            for cp in page_copies(bb, sl, p):
                cp.start()
            return c
        lax.fori_loop(0, n_pages, body, 0)

    @pl.when(b == 0)
    def _():
        issue(0, 0)

    @pl.when(b + 1 < nb)
    def _():
        issue(b + 1, 1 - slot)

    def wait_body(p, c):
        for cp in page_copies(b, slot, p):
            cp.wait()
        return c
    lax.fori_loop(0, n_pages, wait_body, 0)

    q = q_ref[0]
    ql, qr_even, qr_odd = q[:, :half], q[:, half:half + LANES], q[:, half + LANES:]
    n_packed = cbuf.shape[1]
    ch = 1024
    for c in range(n_packed // ch):
        rs = pl.ds(c * ch, ch)
        ce = cbuf[slot, rs, :half].astype(BF16)
        co = cbuf[slot, rs, half:].astype(BF16)
        r = rbuf[slot, rs, :].astype(BF16)
        kb[rs, :half] = ce
        kb[rs, half:] = co
        s_ref[:, (2 * c) * ch:(2 * c + 1) * ch] = _dot_nt(ql, ce) + _dot_nt(qr_even, r)
        s_ref[:, (2 * c + 1) * ch:(2 * c + 2) * ch] = _dot_nt(ql, co) + _dot_nt(qr_odd, r)

    knew = knew_ref[0]
    sn = _dot_nt(q[:, :QK_WIDTH], knew)
    qi = lax.broadcasted_iota(jnp.int32, sn.shape, 0) & (n_new - 1)
    ki = lax.broadcasted_iota(jnp.int32, sn.shape, 1)
    sn = jnp.where((ki < n_new) & (ki <= qi), sn, NEG_INF)
    s = s_ref[...]
    m = jnp.maximum(jnp.max(s, axis=-1, keepdims=True), jnp.max(sn, axis=-1, keepdims=True))
    pn = jnp.exp(sn - m)
    s_ref[...] = jnp.exp(s - m)
    l = jnp.sum(s_ref[...], axis=-1, keepdims=True) + jnp.sum(pn, axis=-1, keepdims=True)
    o = _dot(pn.astype(BF16), knew[:, :half])
    for c in range(n_packed // ch):
        rs = pl.ds(c * ch, ch)
        o += _dot(s_ref[:, (2 * c) * ch:(2 * c + 1) * ch].astype(BF16), kb[rs, :half])
        o += _dot(s_ref[:, (2 * c + 1) * ch:(2 * c + 2) * ch].astype(BF16), kb[rs, half:])
    o_ref[0] = (o / l).astype(BF16)


def _sample_attn(page_table, qs, knew, ckv_pages, kr_pages, *, n_new):
    nb, rows, qw = qs.shape
    n_pages = page_table.shape[1]
    prow, cw = ckv_pages.shape[1], ckv_pages.shape[2]
    n_packed = n_pages * prow
    grid_spec = pltpu.PrefetchScalarGridSpec(
        num_scalar_prefetch=1,
        grid=(nb,),
        in_specs=[pl.BlockSpec((1, rows, qw), lambda b, pt: (b, 0, 0)),
                  pl.BlockSpec((1,) + knew.shape[1:], lambda b, pt: (b, 0, 0)),
                  pl.BlockSpec(memory_space=pl.ANY), pl.BlockSpec(memory_space=pl.ANY)],
        out_specs=pl.BlockSpec((1, rows, cw // 2), lambda b, pt: (b, 0, 0)),
        scratch_shapes=[pltpu.VMEM((2, n_packed, cw), F32), pltpu.VMEM((2, n_packed, LANES), F32),
                        pltpu.VMEM((n_packed, cw), BF16), pltpu.VMEM((rows, 2 * n_packed), F32),
                        pltpu.SemaphoreType.DMA((2, 2))],
    )
    return pl.pallas_call(
        functools.partial(_sample_attn_kernel, n_pages=n_pages, n_new=n_new),
        grid_spec=grid_spec,
        out_shape=jax.ShapeDtypeStruct((nb, rows, cw // 2), BF16),
        compiler_params=_params("arbitrary"),
        name="sample_attn",
    )(page_table, qs, knew, ckv_pages, kr_pages)


def _mla_out_kernel(h_ref, op_ref, os_ref, wuv_ref, wo_ref, out_ref, *, n_prompt_tiles):
    is_sample = pl.program_id(0) >= n_prompt_tiles
    nh = op_ref.shape[0]
    heads = []
    for hd in range(nh):
        o = jnp.where(is_sample, os_ref[hd], op_ref[hd])
        heads.append(_dot(o, wuv_ref[hd]).astype(BF16))
    out_ref[...] = h_ref[...] + _dot(jnp.concatenate(heads, axis=-1), wo_ref[...])


def _mla_out(h, o_p, o_s, w_uv, w_o, *, tp):
    t, d = h.shape
    tm = TOKEN_TILE
    npt = tp // tm
    nh = w_uv.shape[0]
    return pl.pallas_call(
        functools.partial(_mla_out_kernel, n_prompt_tiles=npt),
        grid=(t // tm,),
        in_specs=[pl.BlockSpec((tm, d), lambda i: (i, 0)),
                  pl.BlockSpec((nh, tm, 256), lambda i: (0, jnp.minimum(i, npt - 1), 0)),
                  pl.BlockSpec((nh, tm, 256), lambda i: (0, jnp.maximum(i - npt, 0), 0)),
                  pl.BlockSpec(w_uv.shape, lambda i: (0, 0, 0)),
                  pl.BlockSpec(w_o.shape, lambda i: (0, 0))],
        out_specs=pl.BlockSpec((tm, d), lambda i: (i, 0)),
        out_shape=jax.ShapeDtypeStruct((t, d), F32),
        compiler_params=_params("arbitrary"),
        name="mla_out",
    )(h, o_p, o_s, w_uv, w_o)


def _ffn_pre_kernel(h_ref, g_ref, wr_ref, br_ref, text_ref, cnt_ref, carry_ref):
    i = pl.program_id(0)
    tm, d = h_ref.shape
    n_exp = N_EXPERT_GROUPS * EXPERTS_PER_GROUP

    @pl.when(i == 0)
    def _():
        carry_ref[...] = jnp.zeros(carry_ref.shape, F32)

    t = _rms(h_ref[...], g_ref[...])
    text_ref[:, :d] = t
    logits = jnp.dot(t, wr_ref[...], precision=lax.Precision.HIGHEST, preferred_element_type=F32) + br_ref[...]
    lane_i = lax.broadcasted_iota(jnp.int32, (tm, LANES), 1)
    lane = lane_i.astype(F32)
    lane_group = (lane_i >> (EXPERTS_PER_GROUP.bit_length() - 1)).astype(F32)

    def first_lane_of_max(vals, valid):
        top = jnp.max(jnp.where(valid, vals, -jnp.inf), axis=-1, keepdims=True)
        idx = jnp.min(jnp.where(valid & (vals == top), lane, float(LANES)), axis=-1, keepdims=True)
        return top, idx

    def masked_softmax(valid):
        mx = jnp.max(jnp.where(valid, logits, -jnp.inf), axis=-1, keepdims=True)
        e = jnp.where(valid, jnp.exp(logits - mx), 0.0)
        return e / jnp.sum(e, axis=-1, keepdims=True)

    is_group = (lane >= n_exp) & (lane < n_exp + N_EXPERT_GROUPS)
    pg = masked_softmax(is_group)
    pg_top, g_lane = first_lane_of_max(pg, is_group)
    g_idx = g_lane - n_exp
    in_group = (lane < n_exp) & (lane_group == g_idx)
    pe = masked_softmax(in_group)
    p1, i1 = first_lane_of_max(pe, in_group)
    rest = in_group & (lane != i1)
    p2, i2 = first_lane_of_max(pe, rest)
    denom = p1 + p2
    gates = jnp.where(lane == i1, pg_top * p1 / denom, 0.0) + jnp.where(lane == i2, pg_top * p2 / denom, 0.0)

    onehot = (lane == g_idx).astype(F32)
    rr = lax.broadcasted_iota(jnp.int32, (tm, tm), 0)
    cc = lax.broadcasted_iota(jnp.int32, (tm, tm), 1)
    strict_lower = (cc < rr).astype(BF16)
    before = _dot(strict_lower, onehot.astype(BF16)) + carry_ref[0:1, :]
    rank = jnp.sum(onehot * before, axis=-1, keepdims=True)
    carry_ref[...] = carry_ref[...] + jnp.sum(onehot, axis=0, keepdims=True)
    cnt_ref[...] = carry_ref[...]

    route = jnp.where(lane == ROUTE_GIDX_LANE, g_idx, gates)
    route = jnp.where(lane == ROUTE_RANK_LANE, rank, route)
    text_ref[:, d:] = route


def _ffn_pre(h, g_ffn, w_r, b_r):
    t, d = h.shape
    tm = TOKEN_TILE
    return pl.pallas_call(
        _ffn_pre_kernel,
        grid=(t // tm,),
        in_specs=[pl.BlockSpec((tm, d), lambda i: (i, 0)), pl.BlockSpec((1, d), lambda i: (0, 0)),
                  pl.BlockSpec(w_r.shape, lambda i: (0, 0)), pl.BlockSpec((1, LANES), lambda i: (0, 0))],
        out_specs=[pl.BlockSpec((tm, d + LANES), lambda i: (i, 0)), pl.BlockSpec((8, LANES), lambda i: (0, 0))],
        out_shape=[jax.ShapeDtypeStruct((t, d + LANES), F32), jax.ShapeDtypeStruct((8, LANES), F32)],
        scratch_shapes=[pltpu.VMEM((8, LANES), F32)],
        compiler_params=_params("arbitrary"),
        name="ffn_pre",
    )(h, g_ffn, w_r, b_r)


def _moe_ffn_kernel(tg_ref, nv_ref, src_ref, text_hbm, wg_ref, wu_ref, wd_ref, ytok_hbm,
                    xbuf, ybuf, sem):
    k = pl.program_id(0)
    n = nv_ref[k]
    g = tg_ref[k]
    tm = xbuf.shape[0]
    d = ybuf.shape[1]
    hid = wd_ref.shape[1] // EXPERTS_PER_GROUP

    @pl.when(k == 0)
    def _():
        xbuf[...] = jnp.zeros(xbuf.shape, F32)

    def gather(r):
        tok = src_ref[0, 0, r]
        return pltpu.make_async_copy(text_hbm.at[pl.ds(tok, 1)], xbuf.at[pl.ds(r, 1)], sem.at[0])

    def scatter(r):
        tok = src_ref[0, 0, r]
        return pltpu.make_async_copy(ybuf.at[pl.ds(r, 1)], ytok_hbm.at[pl.ds(tok, 1)], sem.at[1])

    def for_valid_rows(fn):
        def body(r, c):
            fn(r)
            return c
        lax.fori_loop(0, n, body, 0)

    @pl.when(n > 0)
    def _():
        for_valid_rows(lambda r: gather(r).start())
        for_valid_rows(lambda r: gather(r).wait())
        x = xbuf[:, :d].astype(BF16)
        route = xbuf[:, d:]
        lane = lax.broadcasted_iota(jnp.int32, route.shape, 1)
        h1 = _dot(x, wg_ref[0])
        h2 = _dot(x, wu_ref[0])
        acts = []
        for e in range(EXPERTS_PER_GROUP):
            sl = slice(e * hid, (e + 1) * hid)
            gate = jnp.sum(jnp.where(lane == g * EXPERTS_PER_GROUP + e, route, 0.0), axis=-1, keepdims=True)
            a1 = h1[:, sl]
            acts.append((a1 * (1.0 / (1.0 + jnp.exp(-a1))) * h2[:, sl] * gate).astype(BF16))
        ybuf[...] = _dot(jnp.concatenate(acts, axis=-1), wd_ref[0])
        for_valid_rows(lambda r: scatter(r).start())
        for_valid_rows(lambda r: scatter(r).wait())


def _moe_ffn(tile_group, tile_valid, src, t_ext, w_gate, w_up, w_down, *, d):
    t = t_ext.shape[0]
    n_tiles, _, tm = src.shape
    grid_spec = pltpu.PrefetchScalarGridSpec(
        num_scalar_prefetch=2,
        grid=(n_tiles,),
        in_specs=[pl.BlockSpec((1, 1, tm), lambda k, tg, nv: (k, 0, 0), memory_space=pltpu.SMEM),
                  pl.BlockSpec(memory_space=pl.ANY),
                  pl.BlockSpec((1,) + w_gate.shape[1:], lambda k, tg, nv: (tg[k], 0, 0)),
                  pl.BlockSpec((1,) + w_up.shape[1:], lambda k, tg, nv: (tg[k], 0, 0)),
                  pl.BlockSpec((1,) + w_down.shape[1:], lambda k, tg, nv: (tg[k], 0, 0))],
        out_specs=pl.BlockSpec(memory_space=pl.ANY),
        scratch_shapes=[pltpu.VMEM((tm, t_ext.shape[1]), F32), pltpu.VMEM((tm, d), F32),
                        pltpu.SemaphoreType.DMA((2,))],
    )
    return pl.pallas_call(
        _moe_ffn_kernel,
        grid_spec=grid_spec,
        out_shape=jax.ShapeDtypeStruct((t, d), F32),
        compiler_params=_params("arbitrary"),
        name="moe_ffn",
    )(tile_group, tile_valid, src, t_ext, w_gate, w_up, w_down)


def _hier_moe(h, g_ffn, w_rg, b_rg, w_re, b_re, w_gate, w_up, w_down):
    t, d = h.shape
    tm = TOKEN_TILE
    ng, epg = N_EXPERT_GROUPS, EXPERTS_PER_GROUP
    n_exp = ng * epg
    hid = w_gate.shape[-1]
    pad = LANES - n_exp - ng
    w_r = jnp.concatenate([w_re, w_rg, jnp.zeros((d, pad), F32)], axis=1)
    b_r = jnp.concatenate([b_re, b_rg, jnp.zeros((pad,), F32)])[None, :]
    t_ext, cnt = _ffn_pre(h, g_ffn[None, :], w_r, b_r)

    g_idx = t_ext[:, d + ROUTE_GIDX_LANE].astype(jnp.int32)
    rank = t_ext[:, d + ROUTE_RANK_LANE].astype(jnp.int32)
    counts = cnt[0, :ng].astype(jnp.int32)
    tiles_per_group = (counts + tm - 1) // tm
    tile_end = jnp.cumsum(tiles_per_group)
    tile_start = tile_end - tiles_per_group
    n_tiles = t // tm + ng
    pos = tile_start[g_idx] * tm + rank
    src = jnp.zeros((n_tiles * tm,), jnp.int32).at[pos].set(jnp.arange(t, dtype=jnp.int32), unique_indices=True)
    tile_id = jnp.arange(n_tiles, dtype=jnp.int32)
    tile_group = jnp.minimum(jnp.sum(tile_id[:, None] >= tile_end[None, :], axis=1), ng - 1).astype(jnp.int32)
    local = tile_id - tile_start[tile_group]
    tile_valid = jnp.where(tile_id < tile_end[-1], jnp.clip(counts[tile_group] - local * tm, 0, tm), 0).astype(jnp.int32)

    wg = w_gate.reshape(ng, epg, d, hid).transpose(0, 2, 1, 3).reshape(ng, d, epg * hid).astype(BF16)
    wu = w_up.reshape(ng, epg, d, hid).transpose(0, 2, 1, 3).reshape(ng, d, epg * hid).astype(BF16)
    wd = w_down.reshape(ng, epg * hid, d).astype(BF16)
    return _moe_ffn(tile_group, tile_valid, src.reshape(n_tiles, 1, tm), t_ext, wg, wu, wd, d=d)


def _ple_kernel(h_ref, y_ref, p_ref, g_ref, wgate_ref, wproj_ref, out_ref):
    h = h_ref[...] + y_ref[...]
    gate_in = _dot(_rms(h, g_ref[...]).astype(BF16), wgate_ref[...])
    gate = 1.0 / (1.0 + jnp.exp(-gate_in))
    out_ref[...] = h + gate * _dot(p_ref[...].astype(BF16), wproj_ref[...])


def _ple_final_kernel(h_ref, y_ref, p_ref, g_ref, wgate_ref, wproj_ref, gfin_ref, outp_ref, outs_ref,
                      *, n_prompt_tiles):
    i = pl.program_id(0)
    h = h_ref[...] + y_ref[...]
    gate_in = _dot(_rms(h, g_ref[...]).astype(BF16), wgate_ref[...])
    gate = 1.0 / (1.0 + jnp.exp(-gate_in))
    y = _rms(h + gate * _dot(p_ref[...].astype(BF16), wproj_ref[...]), gfin_ref[...])

    @pl.when(i < n_prompt_tiles)
    def _():
        outp_ref[...] = y

    @pl.when(i >= n_prompt_tiles)
    def _():
        outs_ref[...] = y


def _ple(h, y, p, g_ple, w_gate, w_proj, g_final=None, *, tp=None, ts=None):
    t, d = h.shape
    tm = TOKEN_TILE
    row = lambda w: pl.BlockSpec((tm, w), lambda i: (i, 0))
    const = lambda *shape: pl.BlockSpec(shape, lambda i: (0,) * len(shape))
    in_specs = [row(d), row(d), row(p.shape[1]), const(1, d), const(*w_gate.shape), const(*w_proj.shape)]
    if g_final is None:
        return pl.pallas_call(
            _ple_kernel, grid=(t // tm,), in_specs=in_specs, out_specs=row(d),
            out_shape=jax.ShapeDtypeStruct((t, d), F32), compiler_params=_params("arbitrary"), name="ple",
        )(h, y, p, g_ple, w_gate, w_proj)
    npt = tp // tm
    return pl.pallas_call(
        functools.partial(_ple_final_kernel, n_prompt_tiles=npt),
        grid=(t // tm,),
        in_specs=in_specs + [const(1, d)],
        out_specs=[pl.BlockSpec((tm, d), lambda i: (jnp.minimum(i, npt - 1), 0)),
                   pl.BlockSpec((tm, d), lambda i: (jnp.maximum(i - npt, 0), 0))],
        out_shape=[jax.ShapeDtypeStruct((tp, d), F32), jax.ShapeDtypeStruct((ts, d), F32)],
        compiler_params=_params("arbitrary"),
        name="ple_final",
    )(h, y, p, g_ple, w_gate, w_proj, g_final)


def _gelu(x):
    return 0.5 * x * (1.0 + jnp.tanh(0.7978845608028654 * (x + 0.044715 * (x * x * x))))


def _sgu_kernel(h_ref, g_ref, win_ref, bin_ref, gv_ref, bv_ref, ws_ref, bs_ref, wout_ref,
                out_ref, vs_ref, vn_ref, *, n_prompt_tiles, n_new):
    i = pl.program_id(0)
    tm = h_ref.shape[0]
    half = gv_ref.shape[1]
    gd = half // SGU_GROUPS
    ck = SGU_CHUNK
    is_sample = i >= n_prompt_tiles
    x = h_ref[...]
    hn = _rms(x, g_ref[...]).astype(BF16)
    v = _gelu(_dot(hn, win_ref[:, half:]) + bin_ref[:, half:])
    mu = jnp.mean(v, axis=-1, keepdims=True)
    vc = v - mu
    var = jnp.mean(vc * vc, axis=-1, keepdims=True)
    vn = vc * lax.rsqrt(var + LN_EPS) * gv_ref[...] + bv_ref[...]
    vn_ref[...] = vn

    @pl.when(is_sample)
    def _():
        vs_ref[...] = vn

    rr = lax.broadcasted_iota(jnp.int32, (ck, ck), 0)
    cc = lax.broadcasted_iota(jnp.int32, (ck, ck), 1)
    shift = n_new.bit_length() - 1
    same_seq = (rr >> shift) == (cc >> shift)
    mask = (cc <= rr) & (same_seq | jnp.logical_not(is_sample))
    acc = jnp.zeros((tm, x.shape[1]), F32)
    for g in range(SGU_GROUPS):
        cols = slice(g * gd, (g + 1) * gd)
        u = _gelu(_dot(hn, win_ref[:, cols]) + bin_ref[:, cols])
        wm = jnp.where(mask, ws_ref[0, g], 0.0).astype(BF16)
        bias = bs_ref[0, :, g:g + 1]
        parts = []
        for c in range(tm // ck):
            vg = vn_ref[c * ck:(c + 1) * ck, cols].astype(BF16)
            parts.append(_dot(wm, vg) + bias)
        s = jnp.concatenate(parts, axis=0)
        acc = acc + _dot((u * s).astype(BF16), wout_ref[cols, :])
    out_ref[...] = x + acc


def _sgu(h, g_mix, w_in, b_in, g_v, b_v, w_s2, b_s2, w_out, *, tp, ts, n_new):
    t, d = h.shape
    tm = SGU_TILE
    npt = tp // tm
    half = g_v.shape[1]
    const = lambda *shape: pl.BlockSpec(shape, lambda i: (0,) * len(shape))
    variant = lambda i: jnp.minimum(i // npt, 1)
    return pl.pallas_call(
        functools.partial(_sgu_kernel, n_prompt_tiles=npt, n_new=n_new),
        grid=(t // tm,),
        in_specs=[pl.BlockSpec((tm, d), lambda i: (i, 0)), const(1, d), const(*w_in.shape), const(1, 2 * half),
                  const(1, half), const(1, half),
                  pl.BlockSpec((1,) + w_s2.shape[1:], lambda i: (variant(i), 0, 0, 0)),
                  pl.BlockSpec((1,) + b_s2.shape[1:], lambda i: (variant(i), 0, 0)),
                  const(*w_out.shape)],
        out_specs=[pl.BlockSpec((tm, d), lambda i: (i, 0)),
                   pl.BlockSpec((tm, half), lambda i: (jnp.maximum(i - npt, 0), 0))],
        out_shape=[jax.ShapeDtypeStruct((t, d), F32), jax.ShapeDtypeStruct((ts, half), F32)],
        scratch_shapes=[pltpu.VMEM((tm, half), F32)],
        compiler_params=_params("arbitrary"),
        name="sgu",
    )(h, g_mix, w_in, b_in, g_v, b_v, w_s2, b_s2, w_out)


def kernel(x_prompt, x_sample, cache_ckv, cache_krope, page_table, p_prompt, p_sample, g_mix, g_ffn, g_ple, g_final, mla_w_in, mla_g_q, mla_g_kv, mla_w_qb, mla_w_uk, mla_w_uv, mla_w_o, sgu_w_in, sgu_b_in, sgu_g_v, sgu_b_v, sgu_w_s, sgu_b_s, sgu_w_out, moe_w_rg, moe_b_rg, moe_w_re, moe_b_re, moe_w_gate, moe_w_up, moe_w_down, ple_w_proj, ple_w_gate):
    bp, seq, d = x_prompt.shape
    bs, n_new, _ = x_sample.shape
    tp, ts = bp * seq, bs * n_new
    t = tp + ts
    depth = g_mix.shape[0]
    assert depth == 2 and tp % TOKEN_TILE == 0 and ts % TOKEN_TILE == 0 and n_new & (n_new - 1) == 0 and n_new <= 16
    h = jnp.concatenate([x_prompt.reshape(tp, d), x_sample.reshape(ts, d)], axis=0)
    p_all = jnp.concatenate([p_prompt.reshape(depth, tp, -1), p_sample.reshape(depth, ts, -1)], axis=1)

    half = QK_ROPE // 2
    swap = lambda w: jnp.concatenate([w[..., half:], w[..., :half]], axis=-1)
    inv_freq = 1.0 / (ROPE_THETA ** (jnp.arange(0, QK_ROPE, 2, dtype=F32) / QK_ROPE))
    pos = jnp.concatenate([jnp.tile(jnp.arange(seq, dtype=jnp.int32), bp),
                           jnp.tile(PAST_LEN + jnp.arange(n_new, dtype=jnp.int32), bs)])
    ang = pos.astype(F32)[:, None] * inv_freq[None, :]
    cos, sin = jnp.cos(ang), jnp.sin(ang)
    tab = jnp.concatenate([cos, cos, -sin, sin], axis=1)

    w_in = mla_w_in[0]
    n_lat = w_in.shape[1] - QK_ROPE
    w_in_ext = jnp.concatenate([w_in, swap(w_in[:, n_lat:])], axis=1).astype(BF16)
    nh = mla_w_uk.shape[2]
    w_qb = mla_w_qb[0].reshape(-1, nh, QK_NOPE + QK_ROPE)
    q_lora = w_qb.shape[0]
    w_qn = w_qb[:, :, :QK_NOPE].reshape(q_lora, nh * QK_NOPE).astype(BF16)
    w_qr = jnp.concatenate([w_qb[:, :, QK_NOPE:], swap(w_qb[:, :, QK_NOPE:])], axis=-1)
    w_qr = w_qr.reshape(q_lora, nh * 2 * QK_ROPE).astype(BF16)
    w_ukT = mla_w_uk[0].transpose(1, 2, 0).astype(BF16)
    w_uv = mla_w_uv[0].transpose(1, 0, 2).astype(BF16)
    w_o = mla_w_o[0].astype(BF16)

    q_all, k_all, ckv_p, kr_p, ckv_s, kr_s = _mla_proj(
        h, tab, g_mix[0][None, :], w_in_ext, mla_g_q[0][None, :], mla_g_kv[0][None, :], w_qn, w_qr, w_ukT,
        tp=tp, ts=ts)
    o_p = _prompt_attn(q_all, k_all, batch=bp, seq=seq)

    lat = ckv_p.shape[1]
    rows = nh * n_new
    qs = q_all[:, tp:, :].reshape(nh, bs, n_new, QK_WIDTH).transpose(1, 0, 2, 3).reshape(bs, rows, QK_WIDTH)
    qs = jnp.concatenate([qs, jnp.zeros((bs, rows, QK_ROPE), BF16), qs[:, :, lat:lat + QK_ROPE]], axis=-1)
    knew = jnp.pad(k_all[tp:].reshape(bs, n_new, QK_WIDTH), ((0, 0), (0, 16 - n_new), (0, 0)))
    page = cache_ckv.shape[2]
    ckv_pages = cache_ckv[0].reshape(-1, page // 2, 2 * lat)
    kr_pages = cache_krope[0].reshape(-1, page // 2, 2 * QK_ROPE)
    o_s = _sample_attn(page_table, qs, knew, ckv_pages, kr_pages, n_new=n_new)
    o_s = o_s.reshape(bs, nh, n_new, lat).transpose(1, 0, 2, 3).reshape(nh, ts, lat)
    h = _mla_out(h, o_p, o_s, w_uv, w_o, tp=tp)

    def ffn_and_ple(h, i, final):
        y = _hier_moe(h, g_ffn[i], moe_w_rg[i], moe_b_rg[i], moe_w_re[i], moe_b_re[i],
                      moe_w_gate[i], moe_w_up[i], moe_w_down[i])
        args = (h, y, p_all[i], g_ple[i][None, :], ple_w_gate[i].astype(BF16), ple_w_proj[i].astype(BF16))
        if final:
            return _ple(*args, g_final[None, :], tp=tp, ts=ts)
        return _ple(*args)

    h = ffn_and_ple(h, 0, False)

    w_s = sgu_w_s[0]
    reps = SGU_CHUNK // n_new
    w_s2 = jnp.stack([w_s, jnp.tile(w_s[:, :n_new, :n_new], (1, reps, reps))])
    b_s = sgu_b_s[0]
    b_s2 = jnp.stack([b_s.T, jnp.tile(b_s[:, :n_new], (1, reps)).T])
    h, v_s = _sgu(h, g_mix[1][None, :], sgu_w_in[0].astype(BF16), sgu_b_in[0][None, :], sgu_g_v[0][None, :],
                  sgu_b_v[0][None, :], w_s2, b_s2, sgu_w_out[0].astype(BF16), tp=tp, ts=ts, n_new=n_new)
    y_p, y_s = ffn_and_ple(h, 1, True)

    return (y_p.reshape(bp, seq, d), y_s.reshape(bs, n_new, d),
            ckv_p.reshape(1, bp, seq, lat), kr_p.reshape(1, bp, seq, QK_ROPE),
            ckv_s.reshape(1, bs, n_new, lat), kr_s.reshape(1, bs, n_new, QK_ROPE),
            v_s.reshape(1, bs, n_new, -1))
```

```python
import functools

import jax
import jax.numpy as jnp
from jax import lax
from jax.experimental import pallas as pl
from jax.experimental.pallas import tpu as pltpu

F32 = jnp.float32
BF16 = jnp.bfloat16

PAST_LEN = 8192
QK_NOPE = 128
QK_ROPE = 64
ROPE_THETA = 10000.0
SOFTMAX_SCALE = (QK_NOPE + QK_ROPE) ** -0.5
SGU_CHUNK = 128
SGU_GROUPS = 8
N_EXPERT_GROUPS = 4
EXPERTS_PER_GROUP = 8
RMS_EPS = 1e-6
LN_EPS = 1e-5
NEG_INF = -1e30

LANES = 128
TOKEN_TILE = 512
SGU_TILE = 256
ATTN_Q_BLOCK = 128
ATTN_KV_CHUNK = 256
QK_WIDTH = 384
ROUTE_GIDX_LANE = 64
ROUTE_RANK_LANE = 65
VMEM_LIMIT = 56 * 1024 * 1024


def _rms(x, g):
    return x * lax.rsqrt(jnp.mean(x * x, axis=-1, keepdims=True) + RMS_EPS) * g


def _dot(a, b):
    return jnp.dot(a, b, preferred_element_type=F32)


def _dot_nt(a, b):
    return lax.dot_general(a, b, (((1,), (1,)), ((), ())), preferred_element_type=F32)


def _params(*sem):
    return pltpu.CompilerParams(dimension_semantics=sem, vmem_limit_bytes=VMEM_LIMIT)


def _mla_proj_kernel(h_ref, tab_ref, gmix_ref, win_ref, gq_ref, gkv_ref, wqn_ref, wqr_ref, wuk_ref,
                     q_ref, k_ref, ckvp_ref, krp_ref, ckvs_ref, krs_ref, *, n_prompt_tiles, n_heads):
    i = pl.program_id(0)
    tm = h_ref.shape[0]
    hn = _rms(h_ref[...], gmix_ref[...]).astype(BF16)
    a = _dot(hn, win_ref[...])
    cq = _rms(a[:, :384], gq_ref[...]).astype(BF16)
    ckv = _rms(a[:, 384:640], gkv_ref[...])
    tab = tab_ref[...]
    low = lax.broadcasted_iota(jnp.int32, (tm, LANES), 1) < QK_ROPE

    def rope(r):
        p = r * tab
        return jnp.where(low, p + pltpu.roll(p, QK_ROPE, axis=1), 0.0)

    kr = rope(a[:, 640:768])
    k_ref[:, :256] = ckv.astype(BF16)
    k_ref[:, 256:] = kr.astype(BF16)

    @pl.when(i < n_prompt_tiles)
    def _():
        ckvp_ref[...] = ckv
        krp_ref[...] = kr[:, :QK_ROPE]

    @pl.when(i >= n_prompt_tiles)
    def _():
        ckvs_ref[...] = ckv
        krs_ref[...] = kr[:, :QK_ROPE]

    qn = _dot(cq, wqn_ref[...]).astype(BF16)
    qr = _dot(cq, wqr_ref[...])
    for hd in range(n_heads):
        sl = slice(hd * LANES, (hd + 1) * LANES)
        q_ref[hd, :, :256] = (_dot(qn[:, sl], wuk_ref[hd]) * SOFTMAX_SCALE).astype(BF16)
        q_ref[hd, :, 256:] = (rope(qr[:, sl]) * SOFTMAX_SCALE).astype(BF16)


def _mla_proj(h, tab, g_mix, w_in, g_q, g_kv, w_qn, w_qr, w_ukT, *, tp, ts):
    t, d = h.shape
    tm = TOKEN_TILE
    nh = w_ukT.shape[0]
    npt = tp // tm
    const = lambda *shape: pl.BlockSpec(shape, lambda i: (0,) * len(shape))
    row = lambda w: pl.BlockSpec((tm, w), lambda i: (i, 0))
    prow = lambda w: pl.BlockSpec((tm, w), lambda i: (jnp.minimum(i, npt - 1), 0))
    srow = lambda w: pl.BlockSpec((tm, w), lambda i: (jnp.maximum(i - npt, 0), 0))
    return pl.pallas_call(
        functools.partial(_mla_proj_kernel, n_prompt_tiles=npt, n_heads=nh),
        grid=(t // tm,),
        in_specs=[row(d), row(LANES), const(1, d), const(*w_in.shape), const(1, 384), const(1, 256),
                  const(*w_qn.shape), const(*w_qr.shape), const(*w_ukT.shape)],
        out_specs=[pl.BlockSpec((nh, tm, QK_WIDTH), lambda i: (0, i, 0)), row(QK_WIDTH),
                   prow(256), prow(QK_ROPE), srow(256), srow(QK_ROPE)],
        out_shape=[jax.ShapeDtypeStruct((nh, t, QK_WIDTH), BF16), jax.ShapeDtypeStruct((t, QK_WIDTH), BF16),
                   jax.ShapeDtypeStruct((tp, 256), F32), jax.ShapeDtypeStruct((tp, QK_ROPE), F32),
                   jax.ShapeDtypeStruct((ts, 256), F32), jax.ShapeDtypeStruct((ts, QK_ROPE), F32)],
        compiler_params=_params("arbitrary"),
        name="mla_proj",
    )(h, tab, g_mix, w_in, g_q, g_kv, w_qn, w_qr, w_ukT)


def _prompt_attn_kernel(q_ref, k_ref, o_ref, s_ref, mx_ref, sum_ref, acc_ref):
    j = pl.program_id(1)
    nh, qb, w = q_ref.shape
    rows = nh * qb
    ck = ATTN_KV_CHUNK
    q = q_ref[...].reshape(rows, w)
    n_full = (j * qb) // ck
    mx_ref[...] = jnp.full(mx_ref.shape, NEG_INF, F32)

    def key_chunk(c):
        return k_ref[pl.ds(pl.multiple_of(c * ck, ck), ck), :]

    def scores(c, masked):
        s = _dot_nt(q, key_chunk(c))
        if masked:
            qpos = j * qb + (lax.broadcasted_iota(jnp.int32, s.shape, 0) & (qb - 1))
            kpos = c * ck + lax.broadcasted_iota(jnp.int32, s.shape, 1)
            s = jnp.where(kpos <= qpos, s, NEG_INF)
        s_ref[c] = s
        mx_ref[...] = jnp.maximum(mx_ref[...], jnp.maximum(s[:, :LANES], s[:, LANES:]))

    def scores_body(c, carry):
        scores(c, False)
        return carry

    lax.fori_loop(0, n_full, scores_body, 0)
    scores(n_full, True)
    mx_ref[...] = jnp.broadcast_to(jnp.max(mx_ref[...], axis=-1, keepdims=True), mx_ref.shape)
    sum_ref[...] = jnp.zeros(sum_ref.shape, F32)
    acc_ref[...] = jnp.zeros(acc_ref.shape, F32)

    def pv_body(c, carry):
        m = mx_ref[...]
        p0 = jnp.exp(s_ref[c, :, :LANES] - m)
        p1 = jnp.exp(s_ref[c, :, LANES:] - m)
        sum_ref[...] += p0 + p1
        p = jnp.concatenate([p0, p1], axis=1).astype(BF16)
        acc_ref[...] += _dot(p, key_chunk(c)[:, :256])
        return carry

    lax.fori_loop(0, n_full + 1, pv_body, 0)
    o = acc_ref[...] / jnp.sum(sum_ref[...], axis=-1, keepdims=True)
    o_ref[...] = o.reshape(nh, qb, 256).astype(BF16)


def _prompt_attn(q, k, *, batch, seq):
    nh = q.shape[0]
    qb = ATTN_Q_BLOCK
    nj = seq // qb
    rows = nh * qb
    return pl.pallas_call(
        _prompt_attn_kernel,
        grid=(batch, nj),
        in_specs=[pl.BlockSpec((nh, qb, QK_WIDTH), lambda b, j: (0, b * nj + j, 0)),
                  pl.BlockSpec((seq, QK_WIDTH), lambda b, j: (b, 0))],
        out_specs=pl.BlockSpec((nh, qb, 256), lambda b, j: (0, b * nj + j, 0)),
        out_shape=jax.ShapeDtypeStruct((nh, batch * seq, 256), BF16),
        scratch_shapes=[pltpu.VMEM((seq // ATTN_KV_CHUNK, rows, ATTN_KV_CHUNK), F32),
                        pltpu.VMEM((rows, LANES), F32), pltpu.VMEM((rows, LANES), F32),
                        pltpu.VMEM((rows, 256), F32)],
        compiler_params=_params("arbitrary", "arbitrary"),
        name="prompt_attn",
    )(q, k)


def _sample_attn_kernel(pt_ref, q_ref, knew_ref, ckv_hbm, kr_hbm, o_ref, cbuf, rbuf, kb, s_ref, sem,
                        *, n_pages, n_new, page_base):
    b = pl.program_id(0)
    nb = pl.num_programs(0)
    slot = b % 2
    half = cbuf.shape[2] // 2
    prow = ckv_hbm.shape[1]

    def page_copies(bb, sl, p):
        page = pt_ref[bb, p] + page_base
        dst = pl.ds(p * prow, prow)
        return (pltpu.make_async_copy(ckv_hbm.at[page], cbuf.at[sl, dst], sem.at[0, sl]),
                pltpu.make_async_copy(kr_hbm.at[page], rbuf.at[sl, dst], sem.at[1, sl]))

    def issue(bb, sl):
        def body(p, c):
            for cp in page_copies(bb, sl, p):
                cp.start()
            return c
        lax.fori_loop(0, n_pages, body, 0)

    @pl.when(b == 0)
    def _():
        issue(0, 0)

    @pl.when(b + 1 < nb)
    def _():
        issue(b + 1, 1 - slot)

    def wait_body(p, c):
        for cp in page_copies(b, slot, p):
            cp.wait()
        return c
    lax.fori_loop(0, n_pages, wait_body, 0)

    q = q_ref[0]
    ql, qr_even, qr_odd = q[:, :half], q[:, half:half + LANES], q[:, half + LANES:]
    n_packed = cbuf.shape[1]
    ch = 1024
    for c in range(n_packed // ch):
        rs = pl.ds(c * ch, ch)
        ce = cbuf[slot, rs, :half].astype(BF16)
        co = cbuf[slot, rs, half:].astype(BF16)
        r = rbuf[slot, rs, :].astype(BF16)
        kb[rs, :half] = ce
        kb[rs, half:] = co
        s_ref[:, (2 * c) * ch:(2 * c + 1) * ch] = _dot_nt(ql, ce) + _dot_nt(qr_even, r)
        s_ref[:, (2 * c + 1) * ch:(2 * c + 2) * ch] = _dot_nt(ql, co) + _dot_nt(qr_odd, r)

    knew = knew_ref[0]
    sn = _dot_nt(q[:, :QK_WIDTH], knew)
    qi = lax.broadcasted_iota(jnp.int32, sn.shape, 0) & (n_new - 1)
    ki = lax.broadcasted_iota(jnp.int32, sn.shape, 1)
    sn = jnp.where((ki < n_new) & (ki <= qi), sn, NEG_INF)
    s = s_ref[...]
    m = jnp.maximum(jnp.max(s, axis=-1, keepdims=True), jnp.max(sn, axis=-1, keepdims=True))
    pn = jnp.exp(sn - m)
    s_ref[...] = jnp.exp(s - m)
    l = jnp.sum(s_ref[...], axis=-1, keepdims=True) + jnp.sum(pn, axis=-1, keepdims=True)
    o = _dot(pn.astype(BF16), knew[:, :half])
    for c in range(n_packed // ch):
        rs = pl.ds(c * ch, ch)
        o += _dot(s_ref[:, (2 * c) * ch:(2 * c + 1) * ch].astype(BF16), kb[rs, :half])
        o += _dot(s_ref[:, (2 * c + 1) * ch:(2 * c + 2) * ch].astype(BF16), kb[rs, half:])
    o_ref[0] = (o / l).astype(BF16)


def _sample_attn(page_table, qs, knew, ckv_pages, kr_pages, *, n_new, page_base):
    nb, rows, qw = qs.shape
    n_pages = page_table.shape[1]
    prow, cw = ckv_pages.shape[1], ckv_pages.shape[2]
    n_packed = n_pages * prow
    grid_spec = pltpu.PrefetchScalarGridSpec(
        num_scalar_prefetch=1,
        grid=(nb,),
        in_specs=[pl.BlockSpec((1, rows, qw), lambda b, pt: (b, 0, 0)),
                  pl.BlockSpec((1,) + knew.shape[1:], lambda b, pt: (b, 0, 0)),
                  pl.BlockSpec(memory_space=pl.ANY), pl.BlockSpec(memory_space=pl.ANY)],
        out_specs=pl.BlockSpec((1, rows, cw // 2), lambda b, pt: (b, 0, 0)),
        scratch_shapes=[pltpu.VMEM((2, n_packed, cw), F32), pltpu.VMEM((2, n_packed, LANES), F32),
                        pltpu.VMEM((n_packed, cw), BF16), pltpu.VMEM((rows, 2 * n_packed), F32),
                        pltpu.SemaphoreType.DMA((2, 2))],
    )
    return pl.pallas_call(
        functools.partial(_sample_attn_kernel, n_pages=n_pages, n_new=n_new, page_base=page_base),
        grid_spec=grid_spec,
        out_shape=jax.ShapeDtypeStruct((nb, rows, cw // 2), BF16),
        compiler_params=_params("arbitrary"),
        name="sample_attn",
    )(page_table, qs, knew, ckv_pages, kr_pages)


def _mla_out_kernel(h_ref, op_ref, os_ref, wuv_ref, wo_ref, out_ref, *, n_prompt_tiles):
    is_sample = pl.program_id(0) >= n_prompt_tiles
    nh = op_ref.shape[0]
    heads = []
    for hd in range(nh):
        o = jnp.where(is_sample, os_ref[hd], op_ref[hd])
        heads.append(_dot(o, wuv_ref[hd]).astype(BF16))
    out_ref[...] = h_ref[...] + _dot(jnp.concatenate(heads, axis=-1), wo_ref[...])


def _mla_out(h, o_p, o_s, w_uv, w_o, *, tp):
    t, d = h.shape
    tm = TOKEN_TILE
    npt = tp // tm
    nh = w_uv.shape[0]
    return pl.pallas_call(
        functools.partial(_mla_out_kernel, n_prompt_tiles=npt),
        grid=(t // tm,),
        in_specs=[pl.BlockSpec((tm, d), lambda i: (i, 0)),
                  pl.BlockSpec((nh, tm, 256), lambda i: (0, jnp.minimum(i, npt - 1), 0)),
                  pl.BlockSpec((nh, tm, 256), lambda i: (0, jnp.maximum(i - npt, 0), 0)),
                  pl.BlockSpec(w_uv.shape, lambda i: (0, 0, 0)),
                  pl.BlockSpec(w_o.shape, lambda i: (0, 0))],
        out_specs=pl.BlockSpec((tm, d), lambda i: (i, 0)),
        out_shape=jax.ShapeDtypeStruct((t, d), F32),
        compiler_params=_params("arbitrary"),
        name="mla_out",
    )(h, o_p, o_s, w_uv, w_o)


def _ffn_pre_kernel(h_ref, g_ref, wr_ref, br_ref, text_ref, cnt_ref, carry_ref):
    i = pl.program_id(0)
    tm, d = h_ref.shape
    n_exp = N_EXPERT_GROUPS * EXPERTS_PER_GROUP

    @pl.when(i == 0)
    def _():
        carry_ref[...] = jnp.zeros(carry_ref.shape, F32)

    t = _rms(h_ref[...], g_ref[...])
    text_ref[:, :d] = t
    logits = jnp.dot(t, wr_ref[...], precision=lax.Precision.HIGHEST, preferred_element_type=F32) + br_ref[...]
    lane_i = lax.broadcasted_iota(jnp.int32, (tm, LANES), 1)
    lane = lane_i.astype(F32)
    lane_group = (lane_i >> (EXPERTS_PER_GROUP.bit_length() - 1)).astype(F32)

    def first_lane_of_max(vals, valid):
        top = jnp.max(jnp.where(valid, vals, -jnp.inf), axis=-1, keepdims=True)
        idx = jnp.min(jnp.where(valid & (vals == top), lane, float(LANES)), axis=-1, keepdims=True)
        return top, idx

    def masked_softmax(valid):
        mx = jnp.max(jnp.where(valid, logits, -jnp.inf), axis=-1, keepdims=True)
        e = jnp.where(valid, jnp.exp(logits - mx), 0.0)
        return e / jnp.sum(e, axis=-1, keepdims=True)

    is_group = (lane >= n_exp) & (lane < n_exp + N_EXPERT_GROUPS)
    pg = masked_softmax(is_group)
    pg_top, g_lane = first_lane_of_max(pg, is_group)
    g_idx = g_lane - n_exp
    in_group = (lane < n_exp) & (lane_group == g_idx)
    pe = masked_softmax(in_group)
    p1, i1 = first_lane_of_max(pe, in_group)
    rest = in_group & (lane != i1)
    p2, i2 = first_lane_of_max(pe, rest)
    denom = p1 + p2
    gates = jnp.where(lane == i1, pg_top * p1 / denom, 0.0) + jnp.where(lane == i2, pg_top * p2 / denom, 0.0)

    onehot = (lane == g_idx).astype(F32)
    rr = lax.broadcasted_iota(jnp.int32, (tm, tm), 0)
    cc = lax.broadcasted_iota(jnp.int32, (tm, tm), 1)
    strict_lower = (cc < rr).astype(BF16)
    before = _dot(strict_lower, onehot.astype(BF16)) + carry_ref[0:1, :]
    rank = jnp.sum(onehot * before, axis=-1, keepdims=True)
    carry_ref[...] = carry_ref[...] + jnp.sum(onehot, axis=0, keepdims=True)
    cnt_ref[...] = carry_ref[...]

    route = jnp.where(lane == ROUTE_GIDX_LANE, g_idx, gates)
    route = jnp.where(lane == ROUTE_RANK_LANE, rank, route)
    text_ref[:, d:] = route


def _ffn_pre(h, g_ffn, w_r, b_r):
    t, d = h.shape
    tm = TOKEN_TILE
    return pl.pallas_call(
        _ffn_pre_kernel,
        grid=(t // tm,),
        in_specs=[pl.BlockSpec((tm, d), lambda i: (i, 0)), pl.BlockSpec((1, d), lambda i: (0, 0)),
                  pl.BlockSpec(w_r.shape, lambda i: (0, 0)), pl.BlockSpec((1, LANES), lambda i: (0, 0))],
        out_specs=[pl.BlockSpec((tm, d + LANES), lambda i: (i, 0)), pl.BlockSpec((8, LANES), lambda i: (0, 0))],
        out_shape=[jax.ShapeDtypeStruct((t, d + LANES), F32), jax.ShapeDtypeStruct((8, LANES), F32)],
        scratch_shapes=[pltpu.VMEM((8, LANES), F32)],
        compiler_params=_params("arbitrary"),
        name="ffn_pre",
    )(h, g_ffn, w_r, b_r)


def _moe_ffn_kernel(tg_ref, src_cur_ref, src_next_ref, dst_prev_ref, dst_cur_ref, text_hbm,
                    wg_ref, wu_ref, wd_ref, ytok_hbm, xbuf, ybuf, sem):
    k = pl.program_id(0)
    nk = pl.num_programs(0)
    g = tg_ref[k]
    slot = k % 2
    other = 1 - slot
    tm, d = ybuf.shape[1], ybuf.shape[2]
    per_expert = tm // EXPERTS_PER_GROUP

    def gather_start(src_ref, sl, r):
        tok = src_ref[0, 0, r]
        pltpu.make_async_copy(text_hbm.at[pl.ds(tok, 1)], xbuf.at[sl, pl.ds(r, 1)], sem.at[0, sl]).start()

    def scatter_start(dst_ref, sl, r):
        tok = dst_ref[0, 0, r]
        pltpu.make_async_copy(ybuf.at[sl, pl.ds(r, 1)], ytok_hbm.at[pl.ds(tok, 1)], sem.at[1, sl]).start()

    def gather_wait(sl):
        pltpu.make_async_copy(text_hbm.at[pl.ds(0, tm)], xbuf.at[sl], sem.at[0, sl]).wait()

    def scatter_wait(sl):
        pltpu.make_async_copy(ybuf.at[sl], ytok_hbm.at[pl.ds(0, tm)], sem.at[1, sl]).wait()

    @pl.when(k == 0)
    def _():
        ybuf[...] = jnp.zeros(ybuf.shape, F32)

        def body(r, c):
            gather_start(src_cur_ref, 0, r)
            return c
        lax.fori_loop(0, tm, body, 0, unroll=8)

    gather_wait(slot)
    x = xbuf[slot, :, :d].astype(BF16)
    route = xbuf[slot, :, d:]
    lane = lax.broadcasted_iota(jnp.int32, route.shape, 1)
    acts = []
    for e in range(EXPERTS_PER_GROUP):
        gate = jnp.sum(jnp.where(lane == g * EXPERTS_PER_GROUP + e, route, 0.0), axis=-1, keepdims=True)
        a1 = _dot(x, wg_ref[0, e])
        a2 = _dot(x, wu_ref[0, e])
        acts.append((a1 * (1.0 / (1.0 + jnp.exp(-a1))) * a2 * gate).astype(BF16))
        for r in range(e * per_expert, (e + 1) * per_expert):
            gather_start(src_next_ref, other, r)
            scatter_start(dst_prev_ref, other, r)
    ybuf[slot] = _dot(jnp.concatenate(acts, axis=-1), wd_ref[0])
    scatter_wait(other)

    @pl.when(k == nk - 1)
    def _():
        def body(r, c):
            scatter_start(dst_cur_ref, slot, r)
            return c
        lax.fori_loop(0, tm, body, 0, unroll=8)
        scatter_wait(slot)
        gather_wait(other)


def _moe_ffn(tile_group, src, dst, t_ext, w_gate, w_up, w_down, *, d, n_dump):
    t = t_ext.shape[0]
    n_tiles, _, tm = src.shape
    src_next = jnp.concatenate([src[1:], jnp.zeros_like(src[:1])], axis=0)
    dst_prev = jnp.concatenate([(t + jnp.arange(tm, dtype=jnp.int32)).reshape(1, 1, tm), dst[:-1]], axis=0)
    smem = lambda: pl.BlockSpec((1, 1, tm), lambda k, tg: (k, 0, 0), memory_space=pltpu.SMEM)
    grid_spec = pltpu.PrefetchScalarGridSpec(
        num_scalar_prefetch=1,
        grid=(n_tiles,),
        in_specs=[smem(), smem(), smem(), smem(),
                  pl.BlockSpec(memory_space=pl.ANY),
                  pl.BlockSpec((1,) + w_gate.shape[1:], lambda k, tg: (tg[k], 0, 0, 0)),
                  pl.BlockSpec((1,) + w_up.shape[1:], lambda k, tg: (tg[k], 0, 0, 0)),
                  pl.BlockSpec((1,) + w_down.shape[1:], lambda k, tg: (tg[k], 0, 0))],
        out_specs=pl.BlockSpec(memory_space=pl.ANY),
        scratch_shapes=[pltpu.VMEM((2, tm, t_ext.shape[1]), F32), pltpu.VMEM((2, tm, d), F32),
                        pltpu.SemaphoreType.DMA((2, 2))],
    )
    return pl.pallas_call(
        _moe_ffn_kernel,
        grid_spec=grid_spec,
        out_shape=jax.ShapeDtypeStruct((t + n_dump, d), F32),
        compiler_params=_params("arbitrary"),
        name="moe_ffn",
    )(tile_group, src, src_next, dst_prev, dst, t_ext, w_gate, w_up, w_down)


def _hier_moe(h, g_ffn, w_rg, b_rg, w_re, b_re, w_gate, w_up, w_down):
    t, d = h.shape
    tm = TOKEN_TILE
    ng, epg = N_EXPERT_GROUPS, EXPERTS_PER_GROUP
    n_exp = ng * epg
    hid = w_gate.shape[-1]
    pad = LANES - n_exp - ng
    w_r = jnp.concatenate([w_re, w_rg, jnp.zeros((d, pad), F32)], axis=1)
    b_r = jnp.concatenate([b_re, b_rg, jnp.zeros((pad,), F32)])[None, :]
    t_ext, cnt = _ffn_pre(h, g_ffn[None, :], w_r, b_r)

    g_idx = t_ext[:, d + ROUTE_GIDX_LANE].astype(jnp.int32)
    rank = t_ext[:, d + ROUTE_RANK_LANE].astype(jnp.int32)
    counts = cnt[0, :ng].astype(jnp.int32)
    tiles_per_group = (counts + tm - 1) // tm
    tile_end = jnp.cumsum(tiles_per_group)
    tile_start = tile_end - tiles_per_group
    n_tiles = t // tm + ng - 1
    pos = tile_start[g_idx] * tm + rank
    slot_id = jnp.arange(n_tiles * tm, dtype=jnp.int32)
    dst = (t + slot_id % tm).at[pos].set(jnp.arange(t, dtype=jnp.int32), unique_indices=True)
    src = jnp.where(dst < t, dst, 0)
    tile_id = jnp.arange(n_tiles, dtype=jnp.int32)
    tile_group = jnp.minimum(jnp.sum(tile_id[:, None] >= tile_end[None, :], axis=1), ng - 1).astype(jnp.int32)

    wg = w_gate.reshape(ng, epg, d, hid).astype(BF16)
    wu = w_up.reshape(ng, epg, d, hid).astype(BF16)
    wd = w_down.reshape(ng, epg * hid, d).astype(BF16)
    return _moe_ffn(tile_group, src.reshape(n_tiles, 1, tm), dst.reshape(n_tiles, 1, tm), t_ext, wg, wu, wd,
                    d=d, n_dump=tm)


def _ple_kernel(h_ref, y_ref, p_ref, g_ref, wgate_ref, wproj_ref, out_ref):
    h = h_ref[...] + y_ref[...]
    gate_in = _dot(_rms(h, g_ref[...]).astype(BF16), wgate_ref[...])
    gate = 1.0 / (1.0 + jnp.exp(-gate_in))
    out_ref[...] = h + gate * _dot(p_ref[...].astype(BF16), wproj_ref[...])


def _ple_final_kernel(h_ref, y_ref, p_ref, g_ref, wgate_ref, wproj_ref, gfin_ref, outp_ref, outs_ref,
                      *, n_prompt_tiles):
    i = pl.program_id(0)
    h = h_ref[...] + y_ref[...]
    gate_in = _dot(_rms(h, g_ref[...]).astype(BF16), wgate_ref[...])
    gate = 1.0 / (1.0 + jnp.exp(-gate_in))
    y = _rms(h + gate * _dot(p_ref[...].astype(BF16), wproj_ref[...]), gfin_ref[...])

    @pl.when(i < n_prompt_tiles)
    def _():
        outp_ref[...] = y

    @pl.when(i >= n_prompt_tiles)
    def _():
        outs_ref[...] = y


def _ple(h, y, p, g_ple, w_gate, w_proj, g_final=None, *, tp=None, ts=None):
    t, d = h.shape
    tm = TOKEN_TILE
    row = lambda w: pl.BlockSpec((tm, w), lambda i: (i, 0))
    const = lambda *shape: pl.BlockSpec(shape, lambda i: (0,) * len(shape))
    in_specs = [row(d), row(d), row(p.shape[1]), const(1, d), const(*w_gate.shape), const(*w_proj.shape)]
    if g_final is None:
        return pl.pallas_call(
            _ple_kernel, grid=(t // tm,), in_specs=in_specs, out_specs=row(d),
            out_shape=jax.ShapeDtypeStruct((t, d), F32), compiler_params=_params("arbitrary"), name="ple",
        )(h, y, p, g_ple, w_gate, w_proj)
    npt = tp // tm
    return pl.pallas_call(
        functools.partial(_ple_final_kernel, n_prompt_tiles=npt),
        grid=(t // tm,),
        in_specs=in_specs + [const(1, d)],
        out_specs=[pl.BlockSpec((tm, d), lambda i: (jnp.minimum(i, npt - 1), 0)),
                   pl.BlockSpec((tm, d), lambda i: (jnp.maximum(i - npt, 0), 0))],
        out_shape=[jax.ShapeDtypeStruct((tp, d), F32), jax.ShapeDtypeStruct((ts, d), F32)],
        compiler_params=_params("arbitrary"),
        name="ple_final",
    )(h, y, p, g_ple, w_gate, w_proj, g_final)


def _gelu(x):
    return 0.5 * x * (1.0 + jnp.tanh(0.7978845608028654 * (x + 0.044715 * (x * x * x))))


def _sgu_kernel(h_ref, g_ref, win_ref, bin_ref, gv_ref, bv_ref, ws_ref, bs_ref, wout_ref,
                out_ref, vs_ref, vn_ref, *, n_prompt_tiles, n_new):
    i = pl.program_id(0)
    tm = h_ref.shape[0]
    half = gv_ref.shape[1]
    gd = half // SGU_GROUPS
    ck = SGU_CHUNK
    is_sample = i >= n_prompt_tiles
    x = h_ref[...]
    hn = _rms(x, g_ref[...]).astype(BF16)
    v = _gelu(_dot(hn, win_ref[:, half:]) + bin_ref[:, half:])
    mu = jnp.mean(v, axis=-1, keepdims=True)
    vc = v - mu
    var = jnp.mean(vc * vc, axis=-1, keepdims=True)
    vn = vc * lax.rsqrt(var + LN_EPS) * gv_ref[...] + bv_ref[...]
    vn_ref[...] = vn

    @pl.when(is_sample)
    def _():
        vs_ref[...] = vn

    rr = lax.broadcasted_iota(jnp.int32, (ck, ck), 0)
    cc = lax.broadcasted_iota(jnp.int32, (ck, ck), 1)
    shift = n_new.bit_length() - 1
    same_seq = (rr >> shift) == (cc >> shift)
    mask = (cc <= rr) & (same_seq | jnp.logical_not(is_sample))
    acc = jnp.zeros((tm, x.shape[1]), F32)
    for g in range(SGU_GROUPS):
        cols = slice(g * gd, (g + 1) * gd)
        u = _gelu(_dot(hn, win_ref[:, cols]) + bin_ref[:, cols])
        wm = jnp.where(mask, ws_ref[0, g], 0.0).astype(BF16)
        bias = bs_ref[0, :, g:g + 1]
        parts = []
        for c in range(tm // ck):
            vg = vn_ref[c * ck:(c + 1) * ck, cols].astype(BF16)
            parts.append(_dot(wm, vg) + bias)
        s = jnp.concatenate(parts, axis=0)
        acc = acc + _dot((u * s).astype(BF16), wout_ref[cols, :])
    out_ref[...] = x + acc


def _sgu(h, g_mix, w_in, b_in, g_v, b_v, w_s2, b_s2, w_out, *, tp, ts, n_new):
    t, d = h.shape
    tm = SGU_TILE
    npt = tp // tm
    half = g_v.shape[1]
    const = lambda *shape: pl.BlockSpec(shape, lambda i: (0,) * len(shape))
    variant = lambda i: jnp.minimum(i // npt, 1)
    return pl.pallas_call(
        functools.partial(_sgu_kernel, n_prompt_tiles=npt, n_new=n_new),
        grid=(t // tm,),
        in_specs=[pl.BlockSpec((tm, d), lambda i: (i, 0)), const(1, d), const(*w_in.shape), const(1, 2 * half),
                  const(1, half), const(1, half),
                  pl.BlockSpec((1,) + w_s2.shape[1:], lambda i: (variant(i), 0, 0, 0)),
                  pl.BlockSpec((1,) + b_s2.shape[1:], lambda i: (variant(i), 0, 0)),
                  const(*w_out.shape)],
        out_specs=[pl.BlockSpec((tm, d), lambda i: (i, 0)),
                   pl.BlockSpec((tm, half), lambda i: (jnp.maximum(i - npt, 0), 0))],
        out_shape=[jax.ShapeDtypeStruct((t, d), F32), jax.ShapeDtypeStruct((ts, half), F32)],
        scratch_shapes=[pltpu.VMEM((tm, half), F32)],
        compiler_params=_params("arbitrary"),
        name="sgu",
    )(h, g_mix, w_in, b_in, g_v, b_v, w_s2, b_s2, w_out)


def kernel(x_prompt, x_sample, cache_ckv, cache_krope, page_table, p_prompt, p_sample, g_mix, g_ffn, g_ple, g_final, mla_w_in, mla_g_q, mla_g_kv, mla_w_qb, mla_w_uk, mla_w_uv, mla_w_o, sgu_w_in, sgu_b_in, sgu_g_v, sgu_b_v, sgu_w_s, sgu_b_s, sgu_w_out, moe_w_rg, moe_b_rg, moe_w_re, moe_b_re, moe_w_gate, moe_w_up, moe_w_down, ple_w_proj, ple_w_gate):
    bp, seq, d = x_prompt.shape
    bs, n_new, _ = x_sample.shape
    tp, ts = bp * seq, bs * n_new
    t = tp + ts
    depth = g_mix.shape[0]
    assert depth == 2 and tp % TOKEN_TILE == 0 and ts % TOKEN_TILE == 0 and n_new & (n_new - 1) == 0 and n_new <= 16
    h = jnp.concatenate([x_prompt.reshape(tp, d), x_sample.reshape(ts, d)], axis=0)
    p_all = jnp.concatenate([p_prompt.reshape(depth, tp, -1), p_sample.reshape(depth, ts, -1)], axis=1)

    half = QK_ROPE // 2
    swap = lambda w: jnp.concatenate([w[..., half:], w[..., :half]], axis=-1)
    inv_freq = 1.0 / (ROPE_THETA ** (jnp.arange(0, QK_ROPE, 2, dtype=F32) / QK_ROPE))
    pos = jnp.concatenate([jnp.tile(jnp.arange(seq, dtype=jnp.int32), bp),
                           jnp.tile(PAST_LEN + jnp.arange(n_new, dtype=jnp.int32), bs)])
    ang = pos.astype(F32)[:, None] * inv_freq[None, :]
    cos, sin = jnp.cos(ang), jnp.sin(ang)
    tab = jnp.concatenate([cos, cos, -sin, sin], axis=1)

    w_in = mla_w_in[0]
    n_lat = w_in.shape[1] - QK_ROPE
    w_in_ext = jnp.concatenate([w_in, swap(w_in[:, n_lat:])], axis=1).astype(BF16)
    nh = mla_w_uk.shape[2]
    w_qb = mla_w_qb[0].reshape(-1, nh, QK_NOPE + QK_ROPE)
    q_lora = w_qb.shape[0]
    w_qn = w_qb[:, :, :QK_NOPE].reshape(q_lora, nh * QK_NOPE).astype(BF16)
    w_qr = jnp.concatenate([w_qb[:, :, QK_NOPE:], swap(w_qb[:, :, QK_NOPE:])], axis=-1)
    w_qr = w_qr.reshape(q_lora, nh * 2 * QK_ROPE).astype(BF16)
    w_ukT = mla_w_uk[0].transpose(1, 2, 0).astype(BF16)
    w_uv = mla_w_uv[0].transpose(1, 0, 2).astype(BF16)
    w_o = mla_w_o[0].astype(BF16)

    q_all, k_all, ckv_p, kr_p, ckv_s, kr_s = _mla_proj(
        h, tab, g_mix[0][None, :], w_in_ext, mla_g_q[0][None, :], mla_g_kv[0][None, :], w_qn, w_qr, w_ukT,
        tp=tp, ts=ts)
    o_p = _prompt_attn(q_all, k_all, batch=bp, seq=seq)

    lat = ckv_p.shape[1]
    rows = nh * n_new
    qs = q_all[:, tp:, :].reshape(nh, bs, n_new, QK_WIDTH).transpose(1, 0, 2, 3).reshape(bs, rows, QK_WIDTH)
    qs = jnp.concatenate([qs, jnp.zeros((bs, rows, QK_ROPE), BF16), qs[:, :, lat:lat + QK_ROPE]], axis=-1)
    knew = jnp.pad(k_all[tp:].reshape(bs, n_new, QK_WIDTH), ((0, 0), (0, 16 - n_new), (0, 0)))
    page = cache_ckv.shape[2]
    attn_layer = 0
    ckv_pages = cache_ckv.reshape(-1, page // 2, 2 * lat)
    kr_pages = cache_krope.reshape(-1, page // 2, 2 * QK_ROPE)
    o_s = _sample_attn(page_table, qs, knew, ckv_pages, kr_pages, n_new=n_new,
                       page_base=attn_layer * cache_ckv.shape[1])
    o_s = o_s.reshape(bs, nh, n_new, lat).transpose(1, 0, 2, 3).reshape(nh, ts, lat)
    h = _mla_out(h, o_p, o_s, w_uv, w_o, tp=tp)

    def ffn_and_ple(h, i, final):
        y = _hier_moe(h, g_ffn[i], moe_w_rg[i], moe_b_rg[i], moe_w_re[i], moe_b_re[i],
                      moe_w_gate[i], moe_w_up[i], moe_w_down[i])
        args = (h, y, p_all[i], g_ple[i][None, :], ple_w_gate[i].astype(BF16), ple_w_proj[i].astype(BF16))
        if final:
            return _ple(*args, g_final[None, :], tp=tp, ts=ts)
        return _ple(*args)

    h = ffn_and_ple(h, 0, False)

    w_s = sgu_w_s[0]
    reps = SGU_CHUNK // n_new
    w_s2 = jnp.stack([w_s, jnp.tile(w_s[:, :n_new, :n_new], (1, reps, reps))])
    b_s = sgu_b_s[0]
    b_s2 = jnp.stack([b_s.T, jnp.tile(b_s[:, :n_new], (1, reps)).T])
    h, v_s = _sgu(h, g_mix[1][None, :], sgu_w_in[0].astype(BF16), sgu_b_in[0][None, :], sgu_g_v[0][None, :],
                  sgu_b_v[0][None, :], w_s2, b_s2, sgu_w_out[0].astype(BF16), tp=tp, ts=ts, n_new=n_new)
    y_p, y_s = ffn_and_ple(h, 1, True)

    return (y_p.reshape(bp, seq, d), y_s.reshape(bs, n_new, d),
            ckv_p.reshape(1, bp, seq, lat), kr_p.reshape(1, bp, seq, QK_ROPE),
            ckv_s.reshape(1, bs, n_new, lat), kr_s.reshape(1, bs, n_new, QK_ROPE),
            v_s.reshape(1, bs, n_new, -1))
```

```python
import functools

import jax
import jax.numpy as jnp
from jax import lax
from jax.experimental import pallas as pl
from jax.experimental.pallas import tpu as pltpu

F32 = jnp.float32
BF16 = jnp.bfloat16

PAST_LEN = 8192
QK_NOPE = 128
QK_ROPE = 64
ROPE_THETA = 10000.0
SOFTMAX_SCALE = (QK_NOPE + QK_ROPE) ** -0.5
SGU_CHUNK = 128
SGU_GROUPS = 8
N_EXPERT_GROUPS = 4
EXPERTS_PER_GROUP = 8
RMS_EPS = 1e-6
LN_EPS = 1e-5
NEG_INF = -1e30

LANES = 128
TOKEN_TILE = 512
SGU_TILE = 256
ATTN_Q_BLOCK = 128
ATTN_KV_CHUNK = 256
QK_WIDTH = 384
ROUTE_GIDX_LANE = 64
ROUTE_RANK_LANE = 65
VMEM_LIMIT = 56 * 1024 * 1024


def _rms(x, g):
    return x * lax.rsqrt(jnp.mean(x * x, axis=-1, keepdims=True) + RMS_EPS) * g


def _dot(a, b):
    return jnp.dot(a, b, preferred_element_type=F32)


def _dot_nt(a, b):
    return lax.dot_general(a, b, (((1,), (1,)), ((), ())), preferred_element_type=F32)


def _params(*sem):
    return pltpu.CompilerParams(dimension_semantics=sem, vmem_limit_bytes=VMEM_LIMIT)


def _mla_proj_kernel(h_ref, tab_ref, gmix_ref, win_ref, gq_ref, gkv_ref, wqn_ref, wqr_ref, wuk_ref,
                     q_ref, k_ref, ckvp_ref, krp_ref, ckvs_ref, krs_ref, *, n_prompt_tiles, n_heads):
    i = pl.program_id(0)
    tm = h_ref.shape[0]
    hn = _rms(h_ref[...], gmix_ref[...]).astype(BF16)
    a = _dot(hn, win_ref[...])
    cq = _rms(a[:, :384], gq_ref[...]).astype(BF16)
    ckv = _rms(a[:, 384:640], gkv_ref[...])
    tab = tab_ref[...]
    low = lax.broadcasted_iota(jnp.int32, (tm, LANES), 1) < QK_ROPE

    def rope(r):
        p = r * tab
        return jnp.where(low, p + pltpu.roll(p, QK_ROPE, axis=1), 0.0)

    kr = rope(a[:, 640:768])
    k_ref[:, :256] = ckv.astype(BF16)
    k_ref[:, 256:] = kr.astype(BF16)

    @pl.when(i < n_prompt_tiles)
    def _():
        ckvp_ref[...] = ckv
        krp_ref[...] = kr[:, :QK_ROPE]

    @pl.when(i >= n_prompt_tiles)
    def _():
        ckvs_ref[...] = ckv
        krs_ref[...] = kr[:, :QK_ROPE]

    qn = _dot(cq, wqn_ref[...]).astype(BF16)
    qr = _dot(cq, wqr_ref[...])
    for hd in range(n_heads):
        sl = slice(hd * LANES, (hd + 1) * LANES)
        q_ref[hd, :, :256] = (_dot(qn[:, sl], wuk_ref[hd]) * SOFTMAX_SCALE).astype(BF16)
        q_ref[hd, :, 256:] = (rope(qr[:, sl]) * SOFTMAX_SCALE).astype(BF16)


def _mla_proj(h, tab, g_mix, w_in, g_q, g_kv, w_qn, w_qr, w_ukT, *, tp, ts):
    t, d = h.shape
    tm = TOKEN_TILE
    nh = w_ukT.shape[0]
    npt = tp // tm
    const = lambda *shape: pl.BlockSpec(shape, lambda i: (0,) * len(shape))
    row = lambda w: pl.BlockSpec((tm, w), lambda i: (i, 0))
    prow = lambda w: pl.BlockSpec((tm, w), lambda i: (jnp.minimum(i, npt - 1), 0))
    srow = lambda w: pl.BlockSpec((tm, w), lambda i: (jnp.maximum(i - npt, 0), 0))
    return pl.pallas_call(
        functools.partial(_mla_proj_kernel, n_prompt_tiles=npt, n_heads=nh),
        grid=(t // tm,),
        in_specs=[row(d), row(LANES), const(1, d), const(*w_in.shape), const(1, 384), const(1, 256),
                  const(*w_qn.shape), const(*w_qr.shape), const(*w_ukT.shape)],
        out_specs=[pl.BlockSpec((nh, tm, QK_WIDTH), lambda i: (0, i, 0)), row(QK_WIDTH),
                   prow(256), prow(QK_ROPE), srow(256), srow(QK_ROPE)],
        out_shape=[jax.ShapeDtypeStruct((nh, t, QK_WIDTH), BF16), jax.ShapeDtypeStruct((t, QK_WIDTH), BF16),
                   jax.ShapeDtypeStruct((tp, 256), F32), jax.ShapeDtypeStruct((tp, QK_ROPE), F32),
                   jax.ShapeDtypeStruct((ts, 256), F32), jax.ShapeDtypeStruct((ts, QK_ROPE), F32)],
        compiler_params=_params("arbitrary"),
        name="mla_proj",
    )(h, tab, g_mix, w_in, g_q, g_kv, w_qn, w_qr, w_ukT)


def _prompt_attn_kernel(q_ref, k_ref, o_ref, s_ref, mx_ref, sum_ref, acc_ref):
    j = pl.program_id(1)
    nh, qb, w = q_ref.shape
    rows = nh * qb
    ck = ATTN_KV_CHUNK
    q = q_ref[...].reshape(rows, w)
    n_full = (j * qb) // ck
    mx_ref[...] = jnp.full(mx_ref.shape, NEG_INF, F32)

    def key_chunk(c):
        return k_ref[pl.ds(pl.multiple_of(c * ck, ck), ck), :]

    def scores(c, masked):
        s = _dot_nt(q, key_chunk(c))
        if masked:
            qpos = j * qb + (lax.broadcasted_iota(jnp.int32, s.shape, 0) & (qb - 1))
            kpos = c * ck + lax.broadcasted_iota(jnp.int32, s.shape, 1)
            s = jnp.where(kpos <= qpos, s, NEG_INF)
        s_ref[c] = s
        mx_ref[...] = jnp.maximum(mx_ref[...], jnp.maximum(s[:, :LANES], s[:, LANES:]))

    def scores_body(c, carry):
        scores(c, False)
        return carry

    lax.fori_loop(0, n_full, scores_body, 0)
    scores(n_full, True)
    mx_ref[...] = jnp.broadcast_to(jnp.max(mx_ref[...], axis=-1, keepdims=True), mx_ref.shape)
    sum_ref[...] = jnp.zeros(sum_ref.shape, F32)
    acc_ref[...] = jnp.zeros(acc_ref.shape, F32)

    def pv_body(c, carry):
        m = mx_ref[...]
        p0 = jnp.exp(s_ref[c, :, :LANES] - m)
        p1 = jnp.exp(s_ref[c, :, LANES:] - m)
        sum_ref[...] += p0 + p1
        p = jnp.concatenate([p0, p1], axis=1).astype(BF16)
        acc_ref[...] += _dot(p, key_chunk(c)[:, :256])
        return carry

    lax.fori_loop(0, n_full + 1, pv_body, 0)
    o = acc_ref[...] / jnp.sum(sum_ref[...], axis=-1, keepdims=True)
    o_ref[...] = o.reshape(nh, qb, 256).astype(BF16)


def _prompt_attn(q, k, *, batch, seq):
    nh = q.shape[0]
    qb = ATTN_Q_BLOCK
    nj = seq // qb
    rows = nh * qb
    return pl.pallas_call(
        _prompt_attn_kernel,
        grid=(batch, nj),
        in_specs=[pl.BlockSpec((nh, qb, QK_WIDTH), lambda b, j: (0, b * nj + j, 0)),
                  pl.BlockSpec((seq, QK_WIDTH), lambda b, j: (b, 0))],
        out_specs=pl.BlockSpec((nh, qb, 256), lambda b, j: (0, b * nj + j, 0)),
        out_shape=jax.ShapeDtypeStruct((nh, batch * seq, 256), BF16),
        scratch_shapes=[pltpu.VMEM((seq // ATTN_KV_CHUNK, rows, ATTN_KV_CHUNK), F32),
                        pltpu.VMEM((rows, LANES), F32), pltpu.VMEM((rows, LANES), F32),
                        pltpu.VMEM((rows, 256), F32)],
        compiler_params=_params("arbitrary", "arbitrary"),
        name="prompt_attn",
    )(q, k)


def _sample_attn_kernel(pt_ref, q_ref, knew_ref, ckv_hbm, krt_hbm, o_ref, cbuf, rbuf, kb, s_ref, sem,
                        *, n_new, page_base):
    b = pl.program_id(0)
    nb = pl.num_programs(0)
    slot = b % 2
    n_pages, page, lat = cbuf.shape[1:]

    def issue(bb, sl):
        def body(p, c):
            pg = pt_ref[bb, p] + page_base
            pltpu.make_async_copy(ckv_hbm.at[pg], cbuf.at[sl, p], sem.at[0, sl]).start()
            pltpu.make_async_copy(krt_hbm.at[pg], rbuf.at[sl, p], sem.at[1, sl]).start()
            return c
        lax.fori_loop(0, n_pages, body, 0, unroll=4)

    @pl.when(b == 0)
    def _():
        issue(0, 0)

    @pl.when(b + 1 < nb)
    def _():
        issue(b + 1, 1 - slot)

    pltpu.make_async_copy(ckv_hbm.at[pl.ds(0, n_pages)], cbuf.at[slot], sem.at[0, slot]).wait()
    pltpu.make_async_copy(krt_hbm.at[pl.ds(0, n_pages)], rbuf.at[slot], sem.at[1, slot]).wait()

    q = q_ref[0]
    ql, qr = q[:, :lat], q[:, lat:lat + QK_ROPE]
    ppc = 8
    ch = ppc * page
    n_chunks = n_pages // ppc
    for c in range(n_chunks):
        cb = cbuf[slot, c * ppc:(c + 1) * ppc].reshape(ch, lat).astype(BF16)
        kb[c * ch:(c + 1) * ch, :] = cb
        rt = jnp.concatenate([rbuf[slot, c * ppc + i] for i in range(ppc)], axis=1).astype(BF16)
        s_ref[:, c * ch:(c + 1) * ch] = _dot_nt(ql, cb) + _dot(qr, rt)

    knew = knew_ref[0]
    sn = _dot_nt(q[:, :QK_WIDTH], knew)
    qi = lax.broadcasted_iota(jnp.int32, sn.shape, 0) & (n_new - 1)
    ki = lax.broadcasted_iota(jnp.int32, sn.shape, 1)
    sn = jnp.where((ki < n_new) & (ki <= qi), sn, NEG_INF)
    s = s_ref[...]
    m = jnp.maximum(jnp.max(s, axis=-1, keepdims=True), jnp.max(sn, axis=-1, keepdims=True))
    pn = jnp.exp(sn - m)
    s_ref[...] = jnp.exp(s - m)
    l = jnp.sum(s_ref[...], axis=-1, keepdims=True) + jnp.sum(pn, axis=-1, keepdims=True)
    o = _dot(pn.astype(BF16), knew[:, :lat])
    for c in range(n_chunks):
        o += _dot(s_ref[:, c * ch:(c + 1) * ch].astype(BF16), kb[c * ch:(c + 1) * ch, :])
    o_ref[0] = (o / l).astype(BF16)


def _sample_attn(page_table, qs, knew, ckv_pages, krt_pages, *, n_new, page_base):
    nb, rows, qw = qs.shape
    n_pages = page_table.shape[1]
    page, lat = ckv_pages.shape[1:]
    rope = krt_pages.shape[1]
    n_keys = n_pages * page
    grid_spec = pltpu.PrefetchScalarGridSpec(
        num_scalar_prefetch=1,
        grid=(nb,),
        in_specs=[pl.BlockSpec((1, rows, qw), lambda b, pt: (b, 0, 0)),
                  pl.BlockSpec((1,) + knew.shape[1:], lambda b, pt: (b, 0, 0)),
                  pl.BlockSpec(memory_space=pl.ANY), pl.BlockSpec(memory_space=pl.ANY)],
        out_specs=pl.BlockSpec((1, rows, lat), lambda b, pt: (b, 0, 0)),
        scratch_shapes=[pltpu.VMEM((2, n_pages, page, lat), F32), pltpu.VMEM((2, n_pages, rope, page), F32),
                        pltpu.VMEM((n_keys, lat), BF16), pltpu.VMEM((rows, n_keys), F32),
                        pltpu.SemaphoreType.DMA((2, 2))],
    )
    return pl.pallas_call(
        functools.partial(_sample_attn_kernel, n_new=n_new, page_base=page_base),
        grid_spec=grid_spec,
        out_shape=jax.ShapeDtypeStruct((nb, rows, lat), BF16),
        compiler_params=_params("arbitrary"),
        name="sample_attn",
    )(page_table, qs, knew, ckv_pages, krt_pages)


def _mla_out_kernel(h_ref, op_ref, os_ref, wuv_ref, wo_ref, out_ref, *, n_prompt_tiles):
    is_sample = pl.program_id(0) >= n_prompt_tiles
    nh = op_ref.shape[0]
    heads = []
    for hd in range(nh):
        o = jnp.where(is_sample, os_ref[hd], op_ref[hd])
        heads.append(_dot(o, wuv_ref[hd]).astype(BF16))
    out_ref[...] = h_ref[...] + _dot(jnp.concatenate(heads, axis=-1), wo_ref[...])


def _mla_out(h, o_p, o_s, w_uv, w_o, *, tp):
    t, d = h.shape
    tm = TOKEN_TILE
    npt = tp // tm
    nh = w_uv.shape[0]
    return pl.pallas_call(
        functools.partial(_mla_out_kernel, n_prompt_tiles=npt),
        grid=(t // tm,),
        in_specs=[pl.BlockSpec((tm, d), lambda i: (i, 0)),
                  pl.BlockSpec((nh, tm, 256), lambda i: (0, jnp.minimum(i, npt - 1), 0)),
                  pl.BlockSpec((nh, tm, 256), lambda i: (0, jnp.maximum(i - npt, 0), 0)),
                  pl.BlockSpec(w_uv.shape, lambda i: (0, 0, 0)),
                  pl.BlockSpec(w_o.shape, lambda i: (0, 0))],
        out_specs=pl.BlockSpec((tm, d), lambda i: (i, 0)),
        out_shape=jax.ShapeDtypeStruct((t, d), F32),
        compiler_params=_params("arbitrary"),
        name="mla_out",
    )(h, o_p, o_s, w_uv, w_o)


def _ffn_pre_kernel(h_ref, g_ref, wr_ref, br_ref, text_ref, cnt_ref, carry_ref):
    i = pl.program_id(0)
    tm, d = h_ref.shape
    n_exp = N_EXPERT_GROUPS * EXPERTS_PER_GROUP

    @pl.when(i == 0)
    def _():
        carry_ref[...] = jnp.zeros(carry_ref.shape, F32)

    t = _rms(h_ref[...], g_ref[...])
    text_ref[:, :d] = t
    logits = jnp.dot(t, wr_ref[...], precision=lax.Precision.HIGHEST, preferred_element_type=F32) + br_ref[...]
    lane_i = lax.broadcasted_iota(jnp.int32, (tm, LANES), 1)
    lane = lane_i.astype(F32)
    lane_group = (lane_i >> (EXPERTS_PER_GROUP.bit_length() - 1)).astype(F32)

    def first_lane_of_max(vals, valid):
        top = jnp.max(jnp.where(valid, vals, -jnp.inf), axis=-1, keepdims=True)
        idx = jnp.min(jnp.where(valid & (vals == top), lane, float(LANES)), axis=-1, keepdims=True)
        return top, idx

    def masked_softmax(valid):
        mx = jnp.max(jnp.where(valid, logits, -jnp.inf), axis=-1, keepdims=True)
        e = jnp.where(valid, jnp.exp(logits - mx), 0.0)
        return e / jnp.sum(e, axis=-1, keepdims=True)

    is_group = (lane >= n_exp) & (lane < n_exp + N_EXPERT_GROUPS)
    pg = masked_softmax(is_group)
    pg_top, g_lane = first_lane_of_max(pg, is_group)
    g_idx = g_lane - n_exp
    in_group = (lane < n_exp) & (lane_group == g_idx)
    pe = masked_softmax(in_group)
    p1, i1 = first_lane_of_max(pe, in_group)
    rest = in_group & (lane != i1)
    p2, i2 = first_lane_of_max(pe, rest)
    denom = p1 + p2
    gates = jnp.where(lane == i1, pg_top * p1 / denom, 0.0) + jnp.where(lane == i2, pg_top * p2 / denom, 0.0)

    onehot = (lane == g_idx).astype(F32)
    rr = lax.broadcasted_iota(jnp.int32, (tm, tm), 0)
    cc = lax.broadcasted_iota(jnp.int32, (tm, tm), 1)
    strict_lower = (cc < rr).astype(BF16)
    before = _dot(strict_lower, onehot.astype(BF16)) + carry_ref[0:1, :]
    rank = jnp.sum(onehot * before, axis=-1, keepdims=True)
    carry_ref[...] = carry_ref[...] + jnp.sum(onehot, axis=0, keepdims=True)
    cnt_ref[...] = carry_ref[...]

    route = jnp.where(lane == ROUTE_GIDX_LANE, g_idx, gates)
    route = jnp.where(lane == ROUTE_RANK_LANE, rank, route)
    text_ref[:, d:] = route


def _ffn_pre(h, g_ffn, w_r, b_r):
    t, d = h.shape
    tm = TOKEN_TILE
    return pl.pallas_call(
        _ffn_pre_kernel,
        grid=(t // tm,),
        in_specs=[pl.BlockSpec((tm, d), lambda i: (i, 0)), pl.BlockSpec((1, d), lambda i: (0, 0)),
                  pl.BlockSpec(w_r.shape, lambda i: (0, 0)), pl.BlockSpec((1, LANES), lambda i: (0, 0))],
        out_specs=[pl.BlockSpec((tm, d + LANES), lambda i: (i, 0)), pl.BlockSpec((8, LANES), lambda i: (0, 0))],
        out_shape=[jax.ShapeDtypeStruct((t, d + LANES), F32), jax.ShapeDtypeStruct((8, LANES), F32)],
        scratch_shapes=[pltpu.VMEM((8, LANES), F32)],
        compiler_params=_params("arbitrary"),
        name="ffn_pre",
    )(h, g_ffn, w_r, b_r)


def _moe_ffn_kernel(tg_ref, src_cur_ref, src_next_ref, dst_prev_ref, dst_cur_ref, text_hbm,
                    wg_ref, wu_ref, wd_ref, ytok_hbm, xbuf, ybuf, sem):
    k = pl.program_id(0)
    nk = pl.num_programs(0)
    g = tg_ref[k]
    slot = k % 2
    other = 1 - slot
    tm, d = ybuf.shape[1], ybuf.shape[2]
    per_expert = tm // EXPERTS_PER_GROUP

    def gather_start(src_ref, sl, r):
        tok = src_ref[0, 0, r]
        pltpu.make_async_copy(text_hbm.at[pl.ds(tok, 1)], xbuf.at[sl, pl.ds(r, 1)], sem.at[0, sl]).start()

    def scatter_start(dst_ref, sl, r):
        tok = dst_ref[0, 0, r]
        pltpu.make_async_copy(ybuf.at[sl, pl.ds(r, 1)], ytok_hbm.at[pl.ds(tok, 1)], sem.at[1, sl]).start()

    def gather_wait(sl):
        pltpu.make_async_copy(text_hbm.at[pl.ds(0, tm)], xbuf.at[sl], sem.at[0, sl]).wait()

    def scatter_wait(sl):
        pltpu.make_async_copy(ybuf.at[sl], ytok_hbm.at[pl.ds(0, tm)], sem.at[1, sl]).wait()

    @pl.when(k == 0)
    def _():
        ybuf[...] = jnp.zeros(ybuf.shape, F32)

        def body(r, c):
            gather_start(src_cur_ref, 0, r)
            return c
        lax.fori_loop(0, tm, body, 0, unroll=8)

    gather_wait(slot)
    x = xbuf[slot, :, :d].astype(BF16)
    route = xbuf[slot, :, d:]
    lane = lax.broadcasted_iota(jnp.int32, route.shape, 1)
    acts = []
    for e in range(EXPERTS_PER_GROUP):
        gate = jnp.sum(jnp.where(lane == g * EXPERTS_PER_GROUP + e, route, 0.0), axis=-1, keepdims=True)
        a1 = _dot(x, wg_ref[0, e])
        a2 = _dot(x, wu_ref[0, e])
        acts.append((a1 * (1.0 / (1.0 + jnp.exp(-a1))) * a2 * gate).astype(BF16))
        for r in range(e * per_expert, (e + 1) * per_expert):
            gather_start(src_next_ref, other, r)
            scatter_start(dst_prev_ref, other, r)
    ybuf[slot] = _dot(jnp.concatenate(acts, axis=-1), wd_ref[0])
    scatter_wait(other)

    @pl.when(k == nk - 1)
    def _():
        def body(r, c):
            scatter_start(dst_cur_ref, slot, r)
            return c
        lax.fori_loop(0, tm, body, 0, unroll=8)
        scatter_wait(slot)
        gather_wait(other)


def _moe_ffn(tile_group, src, dst, t_ext, w_gate, w_up, w_down, *, d, n_dump):
    t = t_ext.shape[0]
    n_tiles, _, tm = src.shape
    src_next = jnp.concatenate([src[1:], jnp.zeros_like(src[:1])], axis=0)
    dst_prev = jnp.concatenate([(t + jnp.arange(tm, dtype=jnp.int32)).reshape(1, 1, tm), dst[:-1]], axis=0)
    smem = lambda: pl.BlockSpec((1, 1, tm), lambda k, tg: (k, 0, 0), memory_space=pltpu.SMEM)
    grid_spec = pltpu.PrefetchScalarGridSpec(
        num_scalar_prefetch=1,
        grid=(n_tiles,),
        in_specs=[smem(), smem(), smem(), smem(),
                  pl.BlockSpec(memory_space=pl.ANY),
                  pl.BlockSpec((1,) + w_gate.shape[1:], lambda k, tg: (tg[k], 0, 0, 0)),
                  pl.BlockSpec((1,) + w_up.shape[1:], lambda k, tg: (tg[k], 0, 0, 0)),
                  pl.BlockSpec((1,) + w_down.shape[1:], lambda k, tg: (tg[k], 0, 0))],
        out_specs=pl.BlockSpec(memory_space=pl.ANY),
        scratch_shapes=[pltpu.VMEM((2, tm, t_ext.shape[1]), F32), pltpu.VMEM((2, tm, d), F32),
                        pltpu.SemaphoreType.DMA((2, 2))],
    )
    return pl.pallas_call(
        _moe_ffn_kernel,
        grid_spec=grid_spec,
        out_shape=jax.ShapeDtypeStruct((t + n_dump, d), F32),
        compiler_params=_params("arbitrary"),
        name="moe_ffn",
    )(tile_group, src, src_next, dst_prev, dst, t_ext, w_gate, w_up, w_down)


def _hier_moe(h, g_ffn, w_rg, b_rg, w_re, b_re, w_gate, w_up, w_down):
    t, d = h.shape
    tm = TOKEN_TILE
    ng, epg = N_EXPERT_GROUPS, EXPERTS_PER_GROUP
    n_exp = ng * epg
    hid = w_gate.shape[-1]
    pad = LANES - n_exp - ng
    w_r = jnp.concatenate([w_re, w_rg, jnp.zeros((d, pad), F32)], axis=1)
    b_r = jnp.concatenate([b_re, b_rg, jnp.zeros((pad,), F32)])[None, :]
    t_ext, cnt = _ffn_pre(h, g_ffn[None, :], w_r, b_r)

    g_idx = t_ext[:, d + ROUTE_GIDX_LANE].astype(jnp.int32)
    rank = t_ext[:, d + ROUTE_RANK_LANE].astype(jnp.int32)
    counts = cnt[0, :ng].astype(jnp.int32)
    tiles_per_group = (counts + tm - 1) // tm
    tile_end = jnp.cumsum(tiles_per_group)
    tile_start = tile_end - tiles_per_group
    n_tiles = t // tm + ng - 1
    pos = tile_start[g_idx] * tm + rank
    slot_id = jnp.arange(n_tiles * tm, dtype=jnp.int32)
    dst = (t + slot_id % tm).at[pos].set(jnp.arange(t, dtype=jnp.int32), unique_indices=True)
    src = jnp.where(dst < t, dst, 0)
    tile_id = jnp.arange(n_tiles, dtype=jnp.int32)
    tile_group = jnp.minimum(jnp.sum(tile_id[:, None] >= tile_end[None, :], axis=1), ng - 1).astype(jnp.int32)

    wg = w_gate.reshape(ng, epg, d, hid).astype(BF16)
    wu = w_up.reshape(ng, epg, d, hid).astype(BF16)
    wd = w_down.reshape(ng, epg * hid, d).astype(BF16)
    return _moe_ffn(tile_group, src.reshape(n_tiles, 1, tm), dst.reshape(n_tiles, 1, tm), t_ext, wg, wu, wd,
                    d=d, n_dump=tm)


def _ple_kernel(h_ref, y_ref, p_ref, g_ref, wgate_ref, wproj_ref, out_ref):
    h = h_ref[...] + y_ref[...]
    gate_in = _dot(_rms(h, g_ref[...]).astype(BF16), wgate_ref[...])
    gate = 1.0 / (1.0 + jnp.exp(-gate_in))
    out_ref[...] = h + gate * _dot(p_ref[...].astype(BF16), wproj_ref[...])


def _ple_final_kernel(h_ref, y_ref, p_ref, g_ref, wgate_ref, wproj_ref, gfin_ref, outp_ref, outs_ref,
                      *, n_prompt_tiles):
    i = pl.program_id(0)
    h = h_ref[...] + y_ref[...]
    gate_in = _dot(_rms(h, g_ref[...]).astype(BF16), wgate_ref[...])
    gate = 1.0 / (1.0 + jnp.exp(-gate_in))
    y = _rms(h + gate * _dot(p_ref[...].astype(BF16), wproj_ref[...]), gfin_ref[...])

    @pl.when(i < n_prompt_tiles)
    def _():
        outp_ref[...] = y

    @pl.when(i >= n_prompt_tiles)
    def _():
        outs_ref[...] = y


def _ple(h, y, p, g_ple, w_gate, w_proj, g_final=None, *, tp=None, ts=None):
    t, d = h.shape
    tm = TOKEN_TILE
    row = lambda w: pl.BlockSpec((tm, w), lambda i: (i, 0))
    const = lambda *shape: pl.BlockSpec(shape, lambda i: (0,) * len(shape))
    in_specs = [row(d), row(d), row(p.shape[1]), const(1, d), const(*w_gate.shape), const(*w_proj.shape)]
    if g_final is None:
        return pl.pallas_call(
            _ple_kernel, grid=(t // tm,), in_specs=in_specs, out_specs=row(d),
            out_shape=jax.ShapeDtypeStruct((t, d), F32), compiler_params=_params("arbitrary"), name="ple",
        )(h, y, p, g_ple, w_gate, w_proj)
    npt = tp // tm
    return pl.pallas_call(
        functools.partial(_ple_final_kernel, n_prompt_tiles=npt),
        grid=(t // tm,),
        in_specs=in_specs + [const(1, d)],
        out_specs=[pl.BlockSpec((tm, d), lambda i: (jnp.minimum(i, npt - 1), 0)),
                   pl.BlockSpec((tm, d), lambda i: (jnp.maximum(i - npt, 0), 0))],
        out_shape=[jax.ShapeDtypeStruct((tp, d), F32), jax.ShapeDtypeStruct((ts, d), F32)],
        compiler_params=_params("arbitrary"),
        name="ple_final",
    )(h, y, p, g_ple, w_gate, w_proj, g_final)


def _gelu(x):
    return 0.5 * x * (1.0 + jnp.tanh(0.7978845608028654 * (x + 0.044715 * (x * x * x))))


def _sgu_kernel(h_ref, g_ref, win_ref, bin_ref, gv_ref, bv_ref, ws_ref, bs_ref, wout_ref,
                out_ref, vs_ref, vn_ref, *, n_prompt_tiles, n_new):
    i = pl.program_id(0)
    tm = h_ref.shape[0]
    half = gv_ref.shape[1]
    gd = half // SGU_GROUPS
    ck = SGU_CHUNK
    is_sample = i >= n_prompt_tiles
    x = h_ref[...]
    hn = _rms(x, g_ref[...]).astype(BF16)
    v = _gelu(_dot(hn, win_ref[:, half:]) + bin_ref[:, half:])
    mu = jnp.mean(v, axis=-1, keepdims=True)
    vc = v - mu
    var = jnp.mean(vc * vc, axis=-1, keepdims=True)
    vn = vc * lax.rsqrt(var + LN_EPS) * gv_ref[...] + bv_ref[...]
    vn_ref[...] = vn

    @pl.when(is_sample)
    def _():
        vs_ref[...] = vn

    rr = lax.broadcasted_iota(jnp.int32, (ck, ck), 0)
    cc = lax.broadcasted_iota(jnp.int32, (ck, ck), 1)
    shift = n_new.bit_length() - 1
    same_seq = (rr >> shift) == (cc >> shift)
    mask = (cc <= rr) & (same_seq | jnp.logical_not(is_sample))
    acc = jnp.zeros((tm, x.shape[1]), F32)
    for g in range(SGU_GROUPS):
        cols = slice(g * gd, (g + 1) * gd)
        u = _gelu(_dot(hn, win_ref[:, cols]) + bin_ref[:, cols])
        wm = jnp.where(mask, ws_ref[0, g], 0.0).astype(BF16)
        bias = bs_ref[0, :, g:g + 1]
        parts = []
        for c in range(tm // ck):
            vg = vn_ref[c * ck:(c + 1) * ck, cols].astype(BF16)
            parts.append(_dot(wm, vg) + bias)
        s = jnp.concatenate(parts, axis=0)
        acc = acc + _dot((u * s).astype(BF16), wout_ref[cols, :])
    out_ref[...] = x + acc


def _sgu(h, g_mix, w_in, b_in, g_v, b_v, w_s2, b_s2, w_out, *, tp, ts, n_new):
    t, d = h.shape
    tm = SGU_TILE
    npt = tp // tm
    half = g_v.shape[1]
    const = lambda *shape: pl.BlockSpec(shape, lambda i: (0,) * len(shape))
    variant = lambda i: jnp.minimum(i // npt, 1)
    return pl.pallas_call(
        functools.partial(_sgu_kernel, n_prompt_tiles=npt, n_new=n_new),
        grid=(t // tm,),
        in_specs=[pl.BlockSpec((tm, d), lambda i: (i, 0)), const(1, d), const(*w_in.shape), const(1, 2 * half),
                  const(1, half), const(1, half),
                  pl.BlockSpec((1,) + w_s2.shape[1:], lambda i: (variant(i), 0, 0, 0)),
                  pl.BlockSpec((1,) + b_s2.shape[1:], lambda i: (variant(i), 0, 0)),
                  const(*w_out.shape)],
        out_specs=[pl.BlockSpec((tm, d), lambda i: (i, 0)),
                   pl.BlockSpec((tm, half), lambda i: (jnp.maximum(i - npt, 0), 0))],
        out_shape=[jax.ShapeDtypeStruct((t, d), F32), jax.ShapeDtypeStruct((ts, half), F32)],
        scratch_shapes=[pltpu.VMEM((tm, half), F32)],
        compiler_params=_params("arbitrary"),
        name="sgu",
    )(h, g_mix, w_in, b_in, g_v, b_v, w_s2, b_s2, w_out)


def kernel(x_prompt, x_sample, cache_ckv, cache_krope, page_table, p_prompt, p_sample, g_mix, g_ffn, g_ple, g_final, mla_w_in, mla_g_q, mla_g_kv, mla_w_qb, mla_w_uk, mla_w_uv, mla_w_o, sgu_w_in, sgu_b_in, sgu_g_v, sgu_b_v, sgu_w_s, sgu_b_s, sgu_w_out, moe_w_rg, moe_b_rg, moe_w_re, moe_b_re, moe_w_gate, moe_w_up, moe_w_down, ple_w_proj, ple_w_gate):
    bp, seq, d = x_prompt.shape
    bs, n_new, _ = x_sample.shape
    tp, ts = bp * seq, bs * n_new
    t = tp + ts
    depth = g_mix.shape[0]
    assert depth == 2 and tp % TOKEN_TILE == 0 and ts % TOKEN_TILE == 0 and n_new & (n_new - 1) == 0 and n_new <= 16
    h = jnp.concatenate([x_prompt.reshape(tp, d), x_sample.reshape(ts, d)], axis=0)
    p_all = jnp.concatenate([p_prompt.reshape(depth, tp, -1), p_sample.reshape(depth, ts, -1)], axis=1)

    half = QK_ROPE // 2
    swap = lambda w: jnp.concatenate([w[..., half:], w[..., :half]], axis=-1)
    inv_freq = 1.0 / (ROPE_THETA ** (jnp.arange(0, QK_ROPE, 2, dtype=F32) / QK_ROPE))
    pos = jnp.concatenate([jnp.tile(jnp.arange(seq, dtype=jnp.int32), bp),
                           jnp.tile(PAST_LEN + jnp.arange(n_new, dtype=jnp.int32), bs)])
    ang = pos.astype(F32)[:, None] * inv_freq[None, :]
    cos, sin = jnp.cos(ang), jnp.sin(ang)
    tab = jnp.concatenate([cos, cos, -sin, sin], axis=1)

    w_in = mla_w_in[0]
    n_lat = w_in.shape[1] - QK_ROPE
    w_in_ext = jnp.concatenate([w_in, swap(w_in[:, n_lat:])], axis=1).astype(BF16)
    nh = mla_w_uk.shape[2]
    w_qb = mla_w_qb[0].reshape(-1, nh, QK_NOPE + QK_ROPE)
    q_lora = w_qb.shape[0]
    w_qn = w_qb[:, :, :QK_NOPE].reshape(q_lora, nh * QK_NOPE).astype(BF16)
    w_qr = jnp.concatenate([w_qb[:, :, QK_NOPE:], swap(w_qb[:, :, QK_NOPE:])], axis=-1)
    w_qr = w_qr.reshape(q_lora, nh * 2 * QK_ROPE).astype(BF16)
    w_ukT = mla_w_uk[0].transpose(1, 2, 0).astype(BF16)
    w_uv = mla_w_uv[0].transpose(1, 0, 2).astype(BF16)
    w_o = mla_w_o[0].astype(BF16)

    q_all, k_all, ckv_p, kr_p, ckv_s, kr_s = _mla_proj(
        h, tab, g_mix[0][None, :], w_in_ext, mla_g_q[0][None, :], mla_g_kv[0][None, :], w_qn, w_qr, w_ukT,
        tp=tp, ts=ts)
    o_p = _prompt_attn(q_all, k_all, batch=bp, seq=seq)

    lat = ckv_p.shape[1]
    rows = nh * n_new
    qs = q_all[:, tp:, :].reshape(nh, bs, n_new, QK_WIDTH).transpose(1, 0, 2, 3).reshape(bs, rows, QK_WIDTH)
    knew = jnp.pad(k_all[tp:].reshape(bs, n_new, QK_WIDTH), ((0, 0), (0, 16 - n_new), (0, 0)))
    page = cache_ckv.shape[2]
    attn_layer = 0
    ckv_pages = cache_ckv.reshape(-1, page, lat)
    krt_pages = cache_krope.transpose(0, 1, 3, 2).reshape(-1, QK_ROPE, page)
    o_s = _sample_attn(page_table, qs, knew, ckv_pages, krt_pages, n_new=n_new,
                       page_base=attn_layer * cache_ckv.shape[1])
    o_s = o_s.reshape(bs, nh, n_new, lat).transpose(1, 0, 2, 3).reshape(nh, ts, lat)
    h = _mla_out(h, o_p, o_s, w_uv, w_o, tp=tp)

    def ffn_and_ple(h, i, final):
        y = _hier_moe(h, g_ffn[i], moe_w_rg[i], moe_b_rg[i], moe_w_re[i], moe_b_re[i],
                      moe_w_gate[i], moe_w_up[i], moe_w_down[i])
        args = (h, y, p_all[i], g_ple[i][None, :], ple_w_gate[i].astype(BF16), ple_w_proj[i].astype(BF16))
        if final:
            return _ple(*args, g_final[None, :], tp=tp, ts=ts)
        return _ple(*args)

    h = ffn_and_ple(h, 0, False)

    w_s = sgu_w_s[0]
    reps = SGU_CHUNK // n_new
    w_s2 = jnp.stack([w_s, jnp.tile(w_s[:, :n_new, :n_new], (1, reps, reps))])
    b_s = sgu_b_s[0]
    b_s2 = jnp.stack([b_s.T, jnp.tile(b_s[:, :n_new], (1, reps)).T])
    h, v_s = _sgu(h, g_mix[1][None, :], sgu_w_in[0].astype(BF16), sgu_b_in[0][None, :], sgu_g_v[0][None, :],
                  sgu_b_v[0][None, :], w_s2, b_s2, sgu_w_out[0].astype(BF16), tp=tp, ts=ts, n_new=n_new)
    y_p, y_s = ffn_and_ple(h, 1, True)

    return (y_p.reshape(bp, seq, d), y_s.reshape(bs, n_new, d),
            ckv_p.reshape(1, bp, seq, lat), kr_p.reshape(1, bp, seq, QK_ROPE),
            ckv_s.reshape(1, bs, n_new, lat), kr_s.reshape(1, bs, n_new, QK_ROPE),
            v_s.reshape(1, bs, n_new, -1))
```

```python
import functools

import jax
import jax.numpy as jnp
from jax import lax
from jax.experimental import pallas as pl
from jax.experimental.pallas import tpu as pltpu

F32 = jnp.float32
BF16 = jnp.bfloat16

PAST_LEN = 8192
QK_NOPE = 128
QK_ROPE = 64
ROPE_THETA = 10000.0
SOFTMAX_SCALE = (QK_NOPE + QK_ROPE) ** -0.5
SGU_CHUNK = 128
SGU_GROUPS = 8
N_EXPERT_GROUPS = 4
EXPERTS_PER_GROUP = 8
RMS_EPS = 1e-6
LN_EPS = 1e-5
NEG_INF = -1e30

LANES = 128
TOKEN_TILE = 512
SGU_TILE = 256
ATTN_Q_BLOCK = 128
ATTN_KV_CHUNK = 256
QK_WIDTH = 384
ROUTE_GIDX_LANE = 64
ROUTE_RANK_LANE = 65
MOE_EXPERT_BLOCK = 2
VMEM_LIMIT = 56 * 1024 * 1024


def _rms(x, g):
    return x * lax.rsqrt(jnp.mean(x * x, axis=-1, keepdims=True) + RMS_EPS) * g


def _dot(a, b):
    return jnp.dot(a, b, preferred_element_type=F32)


def _dot_nt(a, b):
    return lax.dot_general(a, b, (((1,), (1,)), ((), ())), preferred_element_type=F32)


def _params(*sem):
    return pltpu.CompilerParams(dimension_semantics=sem, vmem_limit_bytes=VMEM_LIMIT)


def _mla_proj_kernel(xp_ref, xs_ref, tab_ref, gmix_ref, win_ref, gq_ref, gkv_ref, wqn_ref, wqr_ref, wuk_ref,
                     q_ref, k_ref, ckvp_ref, krp_ref, ckvs_ref, krs_ref, *, n_prompt_tiles, n_heads):
    i = pl.program_id(0)
    tm = xp_ref.shape[0]
    x = jnp.where(i >= n_prompt_tiles, xs_ref[...], xp_ref[...])
    hn = _rms(x, gmix_ref[...]).astype(BF16)
    a = _dot(hn, win_ref[...])
    cq = _rms(a[:, :384], gq_ref[...]).astype(BF16)
    ckv = _rms(a[:, 384:640], gkv_ref[...])
    tab = tab_ref[...]
    low = lax.broadcasted_iota(jnp.int32, (tm, LANES), 1) < QK_ROPE

    def rope(r):
        p = r * tab
        return jnp.where(low, p + pltpu.roll(p, QK_ROPE, axis=1), 0.0)

    kr = rope(a[:, 640:768])
    k_ref[:, :256] = ckv.astype(BF16)
    k_ref[:, 256:] = kr.astype(BF16)

    @pl.when(i < n_prompt_tiles)
    def _():
        ckvp_ref[...] = ckv
        krp_ref[...] = kr[:, :QK_ROPE]

    @pl.when(i >= n_prompt_tiles)
    def _():
        ckvs_ref[...] = ckv
        krs_ref[...] = kr[:, :QK_ROPE]

    qn = _dot(cq, wqn_ref[...]).astype(BF16)
    qr = _dot(cq, wqr_ref[...])
    for hd in range(n_heads):
        sl = slice(hd * LANES, (hd + 1) * LANES)
        q_ref[hd, :, :256] = (_dot(qn[:, sl], wuk_ref[hd]) * SOFTMAX_SCALE).astype(BF16)
        q_ref[hd, :, 256:] = (rope(qr[:, sl]) * SOFTMAX_SCALE).astype(BF16)


def _mla_proj(xp, xs, tab, g_mix, w_in, g_q, g_kv, w_qn, w_qr, w_ukT):
    (tp, d), ts = xp.shape, xs.shape[0]
    t = tp + ts
    tm = TOKEN_TILE
    nh = w_ukT.shape[0]
    npt = tp // tm
    const = lambda *shape: pl.BlockSpec(shape, lambda i: (0,) * len(shape))
    row = lambda w: pl.BlockSpec((tm, w), lambda i: (i, 0))
    prow = lambda w: pl.BlockSpec((tm, w), lambda i: (jnp.minimum(i, npt - 1), 0))
    srow = lambda w: pl.BlockSpec((tm, w), lambda i: (jnp.maximum(i - npt, 0), 0))
    seq_tiles = (tab.shape[0] - ts) // tm
    tab_rows = lambda i: (jnp.where(i < npt, i % seq_tiles, seq_tiles + i - npt), 0)
    return pl.pallas_call(
        functools.partial(_mla_proj_kernel, n_prompt_tiles=npt, n_heads=nh),
        grid=(t // tm,),
        in_specs=[prow(d), srow(d), pl.BlockSpec((tm, LANES), tab_rows),
                  const(1, d), const(*w_in.shape), const(1, 384), const(1, 256),
                  const(*w_qn.shape), const(*w_qr.shape), const(*w_ukT.shape)],
        out_specs=[pl.BlockSpec((nh, tm, QK_WIDTH), lambda i: (0, i, 0)), row(QK_WIDTH),
                   prow(256), prow(QK_ROPE), srow(256), srow(QK_ROPE)],
        out_shape=[jax.ShapeDtypeStruct((nh, t, QK_WIDTH), BF16), jax.ShapeDtypeStruct((t, QK_WIDTH), BF16),
                   jax.ShapeDtypeStruct((tp, 256), F32), jax.ShapeDtypeStruct((tp, QK_ROPE), F32),
                   jax.ShapeDtypeStruct((ts, 256), F32), jax.ShapeDtypeStruct((ts, QK_ROPE), F32)],
        compiler_params=_params("arbitrary"),
        name="mla_proj",
    )(xp, xs, tab, g_mix, w_in, g_q, g_kv, w_qn, w_qr, w_ukT)


def _prompt_attn_kernel(q_ref, k_ref, o_ref, s_ref, mx_ref, sum_ref, acc_ref):
    j = pl.program_id(1)
    nh, qb, w = q_ref.shape
    rows = nh * qb
    ck = ATTN_KV_CHUNK
    q = q_ref[...].reshape(rows, w)
    n_full = (j * qb) // ck
    mx_ref[...] = jnp.full(mx_ref.shape, NEG_INF, F32)

    def key_chunk(c):
        return k_ref[pl.ds(pl.multiple_of(c * ck, ck), ck), :]

    def scores(c, masked):
        s = _dot_nt(q, key_chunk(c))
        if masked:
            qpos = j * qb + (lax.broadcasted_iota(jnp.int32, s.shape, 0) & (qb - 1))
            kpos = c * ck + lax.broadcasted_iota(jnp.int32, s.shape, 1)
            s = jnp.where(kpos <= qpos, s, NEG_INF)
        s_ref[c] = s
        mx_ref[...] = jnp.maximum(mx_ref[...], jnp.maximum(s[:, :LANES], s[:, LANES:]))

    def scores_body(c, carry):
        scores(c, False)
        return carry

    lax.fori_loop(0, n_full, scores_body, 0)
    scores(n_full, True)
    mx_ref[...] = jnp.broadcast_to(jnp.max(mx_ref[...], axis=-1, keepdims=True), mx_ref.shape)
    sum_ref[...] = jnp.zeros(sum_ref.shape, F32)
    acc_ref[...] = jnp.zeros(acc_ref.shape, F32)

    def pv_body(c, carry):
        m = mx_ref[...]
        p0 = jnp.exp(s_ref[c, :, :LANES] - m)
        p1 = jnp.exp(s_ref[c, :, LANES:] - m)
        sum_ref[...] += p0 + p1
        p = jnp.concatenate([p0, p1], axis=1).astype(BF16)
        acc_ref[...] += _dot(p, key_chunk(c)[:, :256])
        return carry

    lax.fori_loop(0, n_full + 1, pv_body, 0)
    o = acc_ref[...] / jnp.sum(sum_ref[...], axis=-1, keepdims=True)
    o_ref[...] = o.reshape(nh, qb, 256).astype(BF16)


def _prompt_attn(q, k, *, batch, seq):
    nh = q.shape[0]
    qb = ATTN_Q_BLOCK
    nj = seq // qb
    rows = nh * qb
    return pl.pallas_call(
        _prompt_attn_kernel,
        grid=(batch, nj),
        in_specs=[pl.BlockSpec((nh, qb, QK_WIDTH), lambda b, j: (0, b * nj + j, 0)),
                  pl.BlockSpec((seq, QK_WIDTH), lambda b, j: (b, 0))],
        out_specs=pl.BlockSpec((nh, qb, 256), lambda b, j: (0, b * nj + j, 0)),
        out_shape=jax.ShapeDtypeStruct((nh, batch * seq, 256), BF16),
        scratch_shapes=[pltpu.VMEM((seq // ATTN_KV_CHUNK, rows, ATTN_KV_CHUNK), F32),
                        pltpu.VMEM((rows, LANES), F32), pltpu.VMEM((rows, LANES), F32),
                        pltpu.VMEM((rows, 256), F32)],
        compiler_params=_params("arbitrary", "arbitrary"),
        name="prompt_attn",
    )(q, k)


def _sample_attn_kernel(pt_ref, q_ref, knew_ref, ckv_hbm, krt_hbm, o_ref, cbuf, rbuf, kb, s_ref, sem,
                        *, n_new, page_base):
    b = pl.program_id(0)
    nb = pl.num_programs(0)
    slot = b % 2
    n_pages, page, lat = cbuf.shape[1:]

    def issue(bb, sl):
        def body(p, c):
            pg = pt_ref[bb, p] + page_base
            pltpu.make_async_copy(ckv_hbm.at[pg], cbuf.at[sl, p], sem.at[0, sl]).start()
            pltpu.make_async_copy(krt_hbm.at[pg], rbuf.at[sl, p], sem.at[1, sl]).start()
            return c
        lax.fori_loop(0, n_pages, body, 0, unroll=4)

    @pl.when(b == 0)
    def _():
        issue(0, 0)

    @pl.when(b + 1 < nb)
    def _():
        issue(b + 1, 1 - slot)

    pltpu.make_async_copy(ckv_hbm.at[pl.ds(0, n_pages)], cbuf.at[slot], sem.at[0, slot]).wait()
    pltpu.make_async_copy(krt_hbm.at[pl.ds(0, n_pages)], rbuf.at[slot], sem.at[1, slot]).wait()

    q = q_ref[0]
    ql, qr = q[:, :lat], q[:, lat:lat + QK_ROPE]
    ppc = 8
    ch = ppc * page
    n_chunks = n_pages // ppc
    for c in range(n_chunks):
        cb = cbuf[slot, c * ppc:(c + 1) * ppc].reshape(ch, lat).astype(BF16)
        kb[c * ch:(c + 1) * ch, :] = cb
        rt = jnp.concatenate([rbuf[slot, c * ppc + i] for i in range(ppc)], axis=1).astype(BF16)
        s_ref[:, c * ch:(c + 1) * ch] = _dot_nt(ql, cb) + _dot(qr, rt)

    knew = knew_ref[0]
    sn = _dot_nt(q[:, :QK_WIDTH], knew)
    qi = lax.broadcasted_iota(jnp.int32, sn.shape, 0) & (n_new - 1)
    ki = lax.broadcasted_iota(jnp.int32, sn.shape, 1)
    sn = jnp.where((ki < n_new) & (ki <= qi), sn, NEG_INF)
    s = s_ref[...]
    m = jnp.maximum(jnp.max(s, axis=-1, keepdims=True), jnp.max(sn, axis=-1, keepdims=True))
    pn = jnp.exp(sn - m)
    s_ref[...] = jnp.exp(s - m)
    l = jnp.sum(s_ref[...], axis=-1, keepdims=True) + jnp.sum(pn, axis=-1, keepdims=True)
    o = _dot(pn.astype(BF16), knew[:, :lat])
    for c in range(n_chunks):
        o += _dot(s_ref[:, c * ch:(c + 1) * ch].astype(BF16), kb[c * ch:(c + 1) * ch, :])
    o_ref[0] = (o / l).astype(BF16)


def _sample_attn(page_table, qs, knew, ckv_pages, krt_pages, *, n_new, page_base):
    nb, rows, qw = qs.shape
    n_pages = page_table.shape[1]
    page, lat = ckv_pages.shape[1:]
    rope = krt_pages.shape[1]
    n_keys = n_pages * page
    grid_spec = pltpu.PrefetchScalarGridSpec(
        num_scalar_prefetch=1,
        grid=(nb,),
        in_specs=[pl.BlockSpec((1, rows, qw), lambda b, pt: (b, 0, 0)),
                  pl.BlockSpec((1,) + knew.shape[1:], lambda b, pt: (b, 0, 0)),
                  pl.BlockSpec(memory_space=pl.ANY), pl.BlockSpec(memory_space=pl.ANY)],
        out_specs=pl.BlockSpec((1, rows, lat), lambda b, pt: (b, 0, 0)),
        scratch_shapes=[pltpu.VMEM((2, n_pages, page, lat), F32), pltpu.VMEM((2, n_pages, rope, page), F32),
                        pltpu.VMEM((n_keys, lat), BF16), pltpu.VMEM((rows, n_keys), F32),
                        pltpu.SemaphoreType.DMA((2, 2))],
    )
    return pl.pallas_call(
        functools.partial(_sample_attn_kernel, n_new=n_new, page_base=page_base),
        grid_spec=grid_spec,
        out_shape=jax.ShapeDtypeStruct((nb, rows, lat), BF16),
        compiler_params=_params("arbitrary"),
        name="sample_attn",
    )(page_table, qs, knew, ckv_pages, krt_pages)


def _mla_out_kernel(xp_ref, xs_ref, op_ref, os_ref, wuv_ref, wo_ref, out_ref, *, n_prompt_tiles):
    is_sample = pl.program_id(0) >= n_prompt_tiles
    nh = op_ref.shape[0]
    heads = []
    for hd in range(nh):
        o = jnp.where(is_sample, os_ref[hd], op_ref[hd])
        heads.append(_dot(o, wuv_ref[hd]).astype(BF16))
    x = jnp.where(is_sample, xs_ref[...], xp_ref[...])
    out_ref[...] = x + _dot(jnp.concatenate(heads, axis=-1), wo_ref[...])


def _mla_out(xp, xs, o_p, o_s, w_uv, w_o):
    (tp, d), ts = xp.shape, xs.shape[0]
    t = tp + ts
    tm = TOKEN_TILE
    npt = tp // tm
    nh = w_uv.shape[0]
    return pl.pallas_call(
        functools.partial(_mla_out_kernel, n_prompt_tiles=npt),
        grid=(t // tm,),
        in_specs=[pl.BlockSpec((tm, d), lambda i: (jnp.minimum(i, npt - 1), 0)),
                  pl.BlockSpec((tm, d), lambda i: (jnp.maximum(i - npt, 0), 0)),
                  pl.BlockSpec((nh, tm, 256), lambda i: (0, jnp.minimum(i, npt - 1), 0)),
                  pl.BlockSpec((nh, tm, 256), lambda i: (0, jnp.maximum(i - npt, 0), 0)),
                  pl.BlockSpec(w_uv.shape, lambda i: (0, 0, 0)),
                  pl.BlockSpec(w_o.shape, lambda i: (0, 0))],
        out_specs=pl.BlockSpec((tm, d), lambda i: (i, 0)),
        out_shape=jax.ShapeDtypeStruct((t, d), F32),
        compiler_params=_params("arbitrary"),
        name="mla_out",
    )(xp, xs, o_p, o_s, w_uv, w_o)


def _ffn_pre_kernel(h_ref, g_ref, wr_ref, br_ref, text_ref, cnt_ref, carry_ref):
    i = pl.program_id(0)
    tm, d = h_ref.shape
    n_exp = N_EXPERT_GROUPS * EXPERTS_PER_GROUP

    @pl.when(i == 0)
    def _():
        carry_ref[...] = jnp.zeros(carry_ref.shape, F32)

    t = _rms(h_ref[...], g_ref[...])
    text_ref[:, :d] = t
    logits = jnp.dot(t, wr_ref[...], precision=lax.Precision.HIGHEST, preferred_element_type=F32) + br_ref[...]
    lane_i = lax.broadcasted_iota(jnp.int32, (tm, LANES), 1)
    lane = lane_i.astype(F32)
    lane_group = (lane_i >> (EXPERTS_PER_GROUP.bit_length() - 1)).astype(F32)

    def first_lane_of_max(vals, valid):
        top = jnp.max(jnp.where(valid, vals, -jnp.inf), axis=-1, keepdims=True)
        idx = jnp.min(jnp.where(valid & (vals == top), lane, float(LANES)), axis=-1, keepdims=True)
        return top, idx

    def masked_softmax(valid):
        mx = jnp.max(jnp.where(valid, logits, -jnp.inf), axis=-1, keepdims=True)
        e = jnp.where(valid, jnp.exp(logits - mx), 0.0)
        return e / jnp.sum(e, axis=-1, keepdims=True)

    is_group = (lane >= n_exp) & (lane < n_exp + N_EXPERT_GROUPS)
    pg = masked_softmax(is_group)
    pg_top, g_lane = first_lane_of_max(pg, is_group)
    g_idx = g_lane - n_exp
    in_group = (lane < n_exp) & (lane_group == g_idx)
    pe = masked_softmax(in_group)
    p1, i1 = first_lane_of_max(pe, in_group)
    rest = in_group & (lane != i1)
    p2, i2 = first_lane_of_max(pe, rest)
    denom = p1 + p2
    gates = jnp.where(lane == i1, pg_top * p1 / denom, 0.0) + jnp.where(lane == i2, pg_top * p2 / denom, 0.0)

    onehot = (lane == g_idx).astype(F32)
    rr = lax.broadcasted_iota(jnp.int32, (tm, tm), 0)
    cc = lax.broadcasted_iota(jnp.int32, (tm, tm), 1)
    strict_lower = (cc < rr).astype(BF16)
    before = _dot(strict_lower, onehot.astype(BF16)) + carry_ref[0:1, :]
    rank = jnp.sum(onehot * before, axis=-1, keepdims=True)
    carry_ref[...] = carry_ref[...] + jnp.sum(onehot, axis=0, keepdims=True)
    cnt_ref[...] = carry_ref[...]

    route = jnp.where(lane == ROUTE_GIDX_LANE, g_idx, gates)
    route = jnp.where(lane == ROUTE_RANK_LANE, rank, route)
    text_ref[:, d:] = route


def _ffn_pre(h, g_ffn, w_r, b_r):
    t, d = h.shape
    tm = TOKEN_TILE
    return pl.pallas_call(
        _ffn_pre_kernel,
        grid=(t // tm,),
        in_specs=[pl.BlockSpec((tm, d), lambda i: (i, 0)), pl.BlockSpec((1, d), lambda i: (0, 0)),
                  pl.BlockSpec(w_r.shape, lambda i: (0, 0)), pl.BlockSpec((1, LANES), lambda i: (0, 0))],
        out_specs=[pl.BlockSpec((tm, d + LANES), lambda i: (i, 0)), pl.BlockSpec((8, LANES), lambda i: (0, 0))],
        out_shape=[jax.ShapeDtypeStruct((t, d + LANES), F32), jax.ShapeDtypeStruct((8, LANES), F32)],
        scratch_shapes=[pltpu.VMEM((8, LANES), F32)],
        compiler_params=_params("arbitrary"),
        name="ffn_pre",
    )(h, g_ffn, w_r, b_r)


def _moe_ffn_kernel(tg_ref, src_cur_ref, src_next_ref, dst_prev_ref, dst_cur_ref, text_hbm,
                    wg_ref, wu_ref, wd_ref, ytok_hbm, xbuf, ybuf, sem):
    k = pl.program_id(0)
    nk = pl.num_programs(0)
    g = tg_ref[k]
    slot = k % 2
    other = 1 - slot
    tm, d = ybuf.shape[1], ybuf.shape[2]
    per_expert = tm // EXPERTS_PER_GROUP

    def gather_start(tok, sl, r):
        pltpu.make_async_copy(text_hbm.at[pl.ds(tok, 1)], xbuf.at[sl, pl.ds(r, 1)], sem.at[0, sl]).start()

    def scatter_start(tok, sl, r):
        pltpu.make_async_copy(ybuf.at[sl, pl.ds(r, 1)], ytok_hbm.at[pl.ds(tok, 1)], sem.at[1, sl]).start()

    def gather_wait(sl):
        pltpu.make_async_copy(text_hbm.at[pl.ds(0, tm)], xbuf.at[sl], sem.at[0, sl]).wait()

    def scatter_wait(sl):
        pltpu.make_async_copy(ybuf.at[sl], ytok_hbm.at[pl.ds(0, tm)], sem.at[1, sl]).wait()

    @pl.when(k == 0)
    def _():
        ybuf[...] = jnp.zeros(ybuf.shape, F32)

        def body(r, c):
            gather_start(src_cur_ref[0, 0, r], 0, r)
            return c
        lax.fori_loop(0, tm, body, 0, unroll=8)

    gather_wait(slot)
    x = xbuf[slot, :, :d].astype(BF16)
    route = xbuf[slot, :, d:]
    lane = lax.broadcasted_iota(jnp.int32, route.shape, 1)
    hid = wg_ref.shape[3]
    for e0 in range(0, EXPERTS_PER_GROUP, MOE_EXPERT_BLOCK):
        for r in range(e0 * per_expert, (e0 + MOE_EXPERT_BLOCK) * per_expert):
            gather_start(src_next_ref[0, 0, r], other, r)
            scatter_start(dst_prev_ref[0, 0, r], other, r)
        acts = []
        for e in range(e0, e0 + MOE_EXPERT_BLOCK):
            gate = jnp.sum(jnp.where(lane == g * EXPERTS_PER_GROUP + e, route, 0.0), axis=-1, keepdims=True)
            a1 = _dot(x, wg_ref[0, e])
            a2 = _dot(x, wu_ref[0, e])
            acts.append((a1 * (1.0 / (1.0 + jnp.exp(-a1))) * a2 * gate).astype(BF16))
        ye = _dot(jnp.concatenate(acts, axis=-1), wd_ref[0, e0 * hid:(e0 + MOE_EXPERT_BLOCK) * hid, :])
        if e0 == 0:
            ybuf[slot] = ye
        else:
            ybuf[slot] += ye
    scatter_wait(other)

    @pl.when(k == nk - 1)
    def _():
        def body(r, c):
            scatter_start(dst_cur_ref[0, 0, r], slot, r)
            return c
        lax.fori_loop(0, tm, body, 0, unroll=8)
        scatter_wait(slot)
        gather_wait(other)


def _moe_ffn(tile_group, src, dst, t_ext, w_gate, w_up, w_down, *, d, n_dump, first_group):
    t = t_ext.shape[0]
    n_tiles, _, tm = src.shape
    src_next = jnp.concatenate([src[1:], jnp.zeros_like(src[:1])], axis=0)
    dst_prev = jnp.concatenate([(t + jnp.arange(tm, dtype=jnp.int32)).reshape(1, 1, tm), dst[:-1]], axis=0)
    smem = lambda: pl.BlockSpec((1, 1, tm), lambda k, tg: (k, 0, 0), memory_space=pltpu.SMEM)
    grid_spec = pltpu.PrefetchScalarGridSpec(
        num_scalar_prefetch=1,
        grid=(n_tiles,),
        in_specs=[smem(), smem(), smem(), smem(),
                  pl.BlockSpec(memory_space=pl.ANY),
                  pl.BlockSpec((1,) + w_gate.shape[1:], lambda k, tg: (first_group + tg[k], 0, 0, 0)),
                  pl.BlockSpec((1,) + w_up.shape[1:], lambda k, tg: (first_group + tg[k], 0, 0, 0)),
                  pl.BlockSpec((1,) + w_down.shape[1:], lambda k, tg: (first_group + tg[k], 0, 0))],
        out_specs=pl.BlockSpec(memory_space=pl.ANY),
        scratch_shapes=[pltpu.VMEM((2, tm, t_ext.shape[1]), F32), pltpu.VMEM((2, tm, d), F32),
                        pltpu.SemaphoreType.DMA((2, 2))],
    )
    return pl.pallas_call(
        _moe_ffn_kernel,
        grid_spec=grid_spec,
        out_shape=jax.ShapeDtypeStruct((t + n_dump, d), F32),
        compiler_params=_params("arbitrary"),
        name="moe_ffn",
    )(tile_group, src, src_next, dst_prev, dst, t_ext, w_gate, w_up, w_down)


def _hier_moe(h, g_ffn, w_rg, b_rg, w_re, b_re, wg, wu, wd, *, layer):
    t, d = h.shape
    tm = TOKEN_TILE
    ng, epg = N_EXPERT_GROUPS, EXPERTS_PER_GROUP
    n_exp = ng * epg
    pad = LANES - n_exp - ng
    w_r = jnp.concatenate([w_re, w_rg, jnp.zeros((d, pad), F32)], axis=1)
    b_r = jnp.concatenate([b_re, b_rg, jnp.zeros((pad,), F32)])[None, :]
    t_ext, cnt = _ffn_pre(h, g_ffn[None, :], w_r, b_r)

    g_idx = t_ext[:, d + ROUTE_GIDX_LANE].astype(jnp.int32)
    rank = t_ext[:, d + ROUTE_RANK_LANE].astype(jnp.int32)
    counts = cnt[0, :ng].astype(jnp.int32)
    tiles_per_group = (counts + tm - 1) // tm
    tile_end = jnp.cumsum(tiles_per_group)
    tile_start = tile_end - tiles_per_group
    n_tiles = t // tm + ng - 1
    pos = tile_start[g_idx] * tm + rank
    slot_id = jnp.arange(n_tiles * tm, dtype=jnp.int32)
    dst = (t + slot_id % tm).at[pos].set(jnp.arange(t, dtype=jnp.int32), unique_indices=True)
    src = jnp.where(dst < t, dst, 0)
    tile_id = jnp.arange(n_tiles, dtype=jnp.int32)
    tile_group = jnp.minimum(jnp.sum(tile_id[:, None] >= tile_end[None, :], axis=1), ng - 1).astype(jnp.int32)
    return _moe_ffn(tile_group, src.reshape(n_tiles, 1, tm), dst.reshape(n_tiles, 1, tm), t_ext, wg, wu, wd,
                    d=d, n_dump=tm, first_group=layer * ng)


def _ple_update(h_ref, y_ref, pp_ref, ps_ref, g_ref, wgate_ref, wproj_ref, is_sample):
    h = h_ref[...] + y_ref[...]
    gate_in = _dot(_rms(h, g_ref[...]).astype(BF16), wgate_ref[...])
    gate = 1.0 / (1.0 + jnp.exp(-gate_in))
    p = jnp.where(is_sample, ps_ref[...], pp_ref[...]).astype(BF16)
    return h + gate * _dot(p, wproj_ref[...])


def _ple_kernel(h_ref, y_ref, pp_ref, ps_ref, g_ref, wgate_ref, wproj_ref, out_ref, *, n_prompt_tiles):
    is_sample = pl.program_id(0) >= n_prompt_tiles
    out_ref[...] = _ple_update(h_ref, y_ref, pp_ref, ps_ref, g_ref, wgate_ref, wproj_ref, is_sample)


def _ple_final_kernel(h_ref, y_ref, pp_ref, ps_ref, g_ref, wgate_ref, wproj_ref, gfin_ref, outp_ref, outs_ref,
                      *, n_prompt_tiles):
    is_sample = pl.program_id(0) >= n_prompt_tiles
    y = _rms(_ple_update(h_ref, y_ref, pp_ref, ps_ref, g_ref, wgate_ref, wproj_ref, is_sample), gfin_ref[...])

    @pl.when(jnp.logical_not(is_sample))
    def _():
        outp_ref[...] = y

    @pl.when(is_sample)
    def _():
        outs_ref[...] = y


def _ple(h, y, p_prompt, p_sample, g_ple, w_gate, w_proj, g_final=None, *, layer, tp, ts):
    t, d = h.shape
    tm = TOKEN_TILE
    npt = tp // tm
    pdim = p_prompt.shape[2]
    row = lambda w: pl.BlockSpec((tm, w), lambda i: (i, 0))
    const = lambda *shape: pl.BlockSpec(shape, lambda i: (0,) * len(shape))
    prompt_rows = lambda i: jnp.minimum(i, npt - 1)
    sample_rows = lambda i: jnp.maximum(i - npt, 0)
    in_specs = [row(d), row(d),
                pl.BlockSpec((None, tm, pdim), lambda i: (layer, prompt_rows(i), 0)),
                pl.BlockSpec((None, tm, pdim), lambda i: (layer, sample_rows(i), 0)),
                const(1, d), const(*w_gate.shape), const(*w_proj.shape)]
    if g_final is None:
        return pl.pallas_call(
            functools.partial(_ple_kernel, n_prompt_tiles=npt), grid=(t // tm,), in_specs=in_specs,
            out_specs=row(d), out_shape=jax.ShapeDtypeStruct((t, d), F32),
            compiler_params=_params("arbitrary"), name="ple",
        )(h, y, p_prompt, p_sample, g_ple, w_gate, w_proj)
    return pl.pallas_call(
        functools.partial(_ple_final_kernel, n_prompt_tiles=npt),
        grid=(t // tm,),
        in_specs=in_specs + [const(1, d)],
        out_specs=[pl.BlockSpec((tm, d), lambda i: (prompt_rows(i), 0)),
                   pl.BlockSpec((tm, d), lambda i: (sample_rows(i), 0))],
        out_shape=[jax.ShapeDtypeStruct((tp, d), F32), jax.ShapeDtypeStruct((ts, d), F32)],
        compiler_params=_params("arbitrary"),
        name="ple_final",
    )(h, y, p_prompt, p_sample, g_ple, w_gate, w_proj, g_final)


def _gelu(x):
    return 0.5 * x * (1.0 + jnp.tanh(0.7978845608028654 * (x + 0.044715 * (x * x * x))))


def _sgu_kernel(h_ref, g_ref, win_ref, bin_ref, gv_ref, bv_ref, ws_ref, bs_ref, wout_ref,
                out_ref, vs_ref, vn_ref, u_ref, us_ref, *, n_prompt_tiles, n_new):
    i = pl.program_id(0)
    tm = h_ref.shape[0]
    half = gv_ref.shape[1]
    gd = half // SGU_GROUPS
    ck = SGU_CHUNK
    is_sample = i >= n_prompt_tiles
    x = h_ref[...]
    hn = _rms(x, g_ref[...]).astype(BF16)
    u_ref[...] = _gelu(_dot(hn, win_ref[:, :half]) + bin_ref[:, :half])
    v = _gelu(_dot(hn, win_ref[:, half:]) + bin_ref[:, half:])
    mu = jnp.mean(v, axis=-1, keepdims=True)
    vc = v - mu
    var = jnp.mean(vc * vc, axis=-1, keepdims=True)
    vn = vc * lax.rsqrt(var + LN_EPS) * gv_ref[...] + bv_ref[...]
    vn_ref[...] = vn

    @pl.when(is_sample)
    def _():
        vs_ref[...] = vn

    rr = lax.broadcasted_iota(jnp.int32, (ck, ck), 0)
    cc = lax.broadcasted_iota(jnp.int32, (ck, ck), 1)
    shift = n_new.bit_length() - 1
    same_seq = (rr >> shift) == (cc >> shift)
    mask = (cc <= rr) & (same_seq | jnp.logical_not(is_sample))
    for g in range(SGU_GROUPS):
        cols = slice(g * gd, (g + 1) * gd)
        wm = jnp.where(mask, ws_ref[0, g], 0.0).astype(BF16)
        bias = bs_ref[0, :, g:g + 1]
        for c in range(tm // ck):
            rows = slice(c * ck, (c + 1) * ck)
            s = _dot(wm, vn_ref[rows, cols].astype(BF16)) + bias
            us_ref[rows, cols] = (u_ref[rows, cols] * s).astype(BF16)
    out_ref[...] = x + _dot(us_ref[...], wout_ref[...])


def _sgu(h, g_mix, w_in, b_in, g_v, b_v, w_s2, b_s2, w_out, *, tp, ts, n_new):
    t, d = h.shape
    tm = SGU_TILE
    npt = tp // tm
    half = g_v.shape[1]
    const = lambda *shape: pl.BlockSpec(shape, lambda i: (0,) * len(shape))
    resident = lambda *shape: pl.BlockSpec(shape, lambda i: (0,) * len(shape), pipeline_mode=pl.Buffered(1))
    variant = lambda i: jnp.minimum(i // npt, 1)
    return pl.pallas_call(
        functools.partial(_sgu_kernel, n_prompt_tiles=npt, n_new=n_new),
        grid=(t // tm,),
        in_specs=[pl.BlockSpec((tm, d), lambda i: (i, 0)), const(1, d), resident(*w_in.shape), const(1, 2 * half),
                  const(1, half), const(1, half),
                  pl.BlockSpec((1,) + w_s2.shape[1:], lambda i: (variant(i), 0, 0, 0)),
                  pl.BlockSpec((1,) + b_s2.shape[1:], lambda i: (variant(i), 0, 0)),
                  resident(*w_out.shape)],
        out_specs=[pl.BlockSpec((tm, d), lambda i: (i, 0)),
                   pl.BlockSpec((tm, half), lambda i: (jnp.maximum(i - npt, 0), 0))],
        out_shape=[jax.ShapeDtypeStruct((t, d), F32), jax.ShapeDtypeStruct((ts, half), F32)],
        scratch_shapes=[pltpu.VMEM((tm, half), F32), pltpu.VMEM((tm, half), F32), pltpu.VMEM((tm, half), BF16)],
        compiler_params=_params("arbitrary"),
        name="sgu",
    )(h, g_mix, w_in, b_in, g_v, b_v, w_s2, b_s2, w_out)


def kernel(x_prompt, x_sample, cache_ckv, cache_krope, page_table, p_prompt, p_sample, g_mix, g_ffn, g_ple, g_final, mla_w_in, mla_g_q, mla_g_kv, mla_w_qb, mla_w_uk, mla_w_uv, mla_w_o, sgu_w_in, sgu_b_in, sgu_g_v, sgu_b_v, sgu_w_s, sgu_b_s, sgu_w_out, moe_w_rg, moe_b_rg, moe_w_re, moe_b_re, moe_w_gate, moe_w_up, moe_w_down, ple_w_proj, ple_w_gate):
    bp, seq, d = x_prompt.shape
    bs, n_new, _ = x_sample.shape
    tp, ts = bp * seq, bs * n_new
    t = tp + ts
    depth = g_mix.shape[0]
    assert depth == 2 and seq % TOKEN_TILE == 0 and ts % TOKEN_TILE == 0 and n_new & (n_new - 1) == 0 and n_new <= 16
    xp, xs = x_prompt.reshape(tp, d), x_sample.reshape(ts, d)
    pp, ps = p_prompt.reshape(depth, tp, -1), p_sample.reshape(depth, ts, -1)

    half = QK_ROPE // 2
    swap = lambda w: jnp.concatenate([w[..., half:], w[..., :half]], axis=-1)
    inv_freq = 1.0 / (ROPE_THETA ** (jnp.arange(0, QK_ROPE, 2, dtype=F32) / QK_ROPE))
    pos = jnp.concatenate([jnp.arange(seq, dtype=jnp.int32),
                           jnp.tile(PAST_LEN + jnp.arange(n_new, dtype=jnp.int32), bs)])
    ang = pos.astype(F32)[:, None] * inv_freq[None, :]
    cos, sin = jnp.cos(ang), jnp.sin(ang)
    tab = jnp.concatenate([cos, cos, -sin, sin], axis=1)

    w_in = mla_w_in[0]
    n_lat = w_in.shape[1] - QK_ROPE
    w_in_ext = jnp.concatenate([w_in, swap(w_in[:, n_lat:])], axis=1).astype(BF16)
    nh = mla_w_uk.shape[2]
    w_qb = mla_w_qb[0].reshape(-1, nh, QK_NOPE + QK_ROPE)
    q_lora = w_qb.shape[0]
    w_qn = w_qb[:, :, :QK_NOPE].reshape(q_lora, nh * QK_NOPE).astype(BF16)
    w_qr = jnp.concatenate([w_qb[:, :, QK_NOPE:], swap(w_qb[:, :, QK_NOPE:])], axis=-1)
    w_qr = w_qr.reshape(q_lora, nh * 2 * QK_ROPE).astype(BF16)
    w_ukT = mla_w_uk[0].transpose(1, 2, 0).astype(BF16)
    w_uv = mla_w_uv[0].transpose(1, 0, 2).astype(BF16)
    w_o = mla_w_o[0].astype(BF16)

    q_all, k_all, ckv_p, kr_p, ckv_s, kr_s = _mla_proj(
        xp, xs, tab, g_mix[0][None, :], w_in_ext, mla_g_q[0][None, :], mla_g_kv[0][None, :], w_qn, w_qr, w_ukT)
    o_p = _prompt_attn(q_all, k_all, batch=bp, seq=seq)

    lat = ckv_p.shape[1]
    rows = nh * n_new
    qs = q_all[:, tp:, :].reshape(nh, bs, n_new, QK_WIDTH).transpose(1, 0, 2, 3).reshape(bs, rows, QK_WIDTH)
    knew = jnp.pad(k_all[tp:].reshape(bs, n_new, QK_WIDTH), ((0, 0), (0, 16 - n_new), (0, 0)))
    page = cache_ckv.shape[2]
    attn_layer = 0
    ckv_pages = cache_ckv.reshape(-1, page, lat)
    krt_pages = cache_krope.transpose(0, 1, 3, 2).reshape(-1, QK_ROPE, page)
    o_s = _sample_attn(page_table, qs, knew, ckv_pages, krt_pages, n_new=n_new,
                       page_base=attn_layer * cache_ckv.shape[1])
    o_s = o_s.reshape(bs, nh, n_new, lat).transpose(1, 0, 2, 3).reshape(nh, ts, lat)
    h = _mla_out(xp, xs, o_p, o_s, w_uv, w_o)

    ng, epg = N_EXPERT_GROUPS, EXPERTS_PER_GROUP
    hid = moe_w_gate.shape[-1]
    wg_all = moe_w_gate.astype(BF16).reshape(depth * ng, epg, d, hid)
    wu_all = moe_w_up.astype(BF16).reshape(depth * ng, epg, d, hid)
    wd_all = moe_w_down.astype(BF16).reshape(depth * ng, epg * hid, d)

    def ffn_and_ple(h, i, final):
        y = _hier_moe(h, g_ffn[i], moe_w_rg[i], moe_b_rg[i], moe_w_re[i], moe_b_re[i],
                      wg_all, wu_all, wd_all, layer=i)
        args = (h, y, pp, ps, g_ple[i][None, :], ple_w_gate[i].astype(BF16), ple_w_proj[i].astype(BF16))
        return _ple(*args, g_final[None, :] if final else None, layer=i, tp=tp, ts=ts)

    h = ffn_and_ple(h, 0, False)

    w_s = sgu_w_s[0]
    reps = SGU_CHUNK // n_new
    w_s2 = jnp.stack([w_s, jnp.tile(w_s[:, :n_new, :n_new], (1, reps, reps))])
    b_s = sgu_b_s[0]
    b_s2 = jnp.stack([b_s.T, jnp.tile(b_s[:, :n_new], (1, reps)).T])
    h, v_s = _sgu(h, g_mix[1][None, :], sgu_w_in[0].astype(BF16), sgu_b_in[0][None, :], sgu_g_v[0][None, :],
                  sgu_b_v[0][None, :], w_s2, b_s2, sgu_w_out[0].astype(BF16), tp=tp, ts=ts, n_new=n_new)
    y_p, y_s = ffn_and_ple(h, 1, True)

    return (y_p.reshape(bp, seq, d), y_s.reshape(bs, n_new, d),
            ckv_p.reshape(1, bp, seq, lat), kr_p.reshape(1, bp, seq, QK_ROPE),
            ckv_s.reshape(1, bs, n_new, lat), kr_s.reshape(1, bs, n_new, QK_ROPE),
            v_s.reshape(1, bs, n_new, -1))
```

```python
import functools

import jax
import jax.numpy as jnp
from jax import lax
from jax.experimental import pallas as pl
from jax.experimental.pallas import tpu as pltpu

F32 = jnp.float32
BF16 = jnp.bfloat16

PAST_LEN = 8192
QK_NOPE = 128
QK_ROPE = 64
ROPE_THETA = 10000.0
SOFTMAX_SCALE = (QK_NOPE + QK_ROPE) ** -0.5
LOG2_E = 1.4426950408889634
SCORE_SCALE = SOFTMAX_SCALE * LOG2_E
SGU_CHUNK = 128
SGU_GROUPS = 8
N_EXPERT_GROUPS = 4
EXPERTS_PER_GROUP = 8
RMS_EPS = 1e-6
LN_EPS = 1e-5
NEG_INF = -1e30

LANES = 128
TOKEN_TILE = 512
SGU_TILE = 256
ATTN_Q_BLOCK = 128
ATTN_KV_CHUNK = 256
QK_WIDTH = 384
ROUTE_GIDX_LANE = 64
ROUTE_RANK_LANE = 65
MOE_EXPERT_BLOCK = 2
VMEM_LIMIT = 56 * 1024 * 1024


def _rms(x, g):
    return x * lax.rsqrt(jnp.mean(x * x, axis=-1, keepdims=True) + RMS_EPS) * g


def _dot(a, b):
    return jnp.dot(a, b, preferred_element_type=F32)


def _dot_nt(a, b):
    return lax.dot_general(a, b, (((1,), (1,)), ((), ())), preferred_element_type=F32)


def _params(*sem):
    return pltpu.CompilerParams(dimension_semantics=sem, vmem_limit_bytes=VMEM_LIMIT)


def _mla_proj_kernel(xp_ref, xs_ref, tab_ref, gmix_ref, win_ref, gq_ref, gkv_ref, wqn_ref, wqr_ref, wuk_ref,
                     q_ref, k_ref, ckvp_ref, krp_ref, ckvs_ref, krs_ref, *, n_prompt_tiles, n_heads):
    i = pl.program_id(0)
    tm = xp_ref.shape[0]
    x = jnp.where(i >= n_prompt_tiles, xs_ref[...], xp_ref[...])
    hn = _rms(x, gmix_ref[...]).astype(BF16)
    a = _dot(hn, win_ref[...])
    cq = _rms(a[:, :384], gq_ref[...]).astype(BF16)
    ckv = _rms(a[:, 384:640], gkv_ref[...])
    tab = tab_ref[...]
    low = lax.broadcasted_iota(jnp.int32, (tm, LANES), 1) < QK_ROPE

    def rope(r):
        p = r * tab
        return jnp.where(low, p + pltpu.roll(p, QK_ROPE, axis=1), 0.0)

    kr = rope(a[:, 640:768])
    k_ref[:, :256] = ckv.astype(BF16)
    k_ref[:, 256:] = kr.astype(BF16)

    @pl.when(i < n_prompt_tiles)
    def _():
        ckvp_ref[...] = ckv
        krp_ref[...] = kr[:, :QK_ROPE]

    @pl.when(i >= n_prompt_tiles)
    def _():
        ckvs_ref[...] = ckv
        krs_ref[...] = kr[:, :QK_ROPE]

    qn = _dot(cq, wqn_ref[...]).astype(BF16)
    qr = _dot(cq, wqr_ref[...])
    for hd in range(n_heads):
        sl = slice(hd * LANES, (hd + 1) * LANES)
        q_ref[hd, :, :256] = (_dot(qn[:, sl], wuk_ref[hd]) * SCORE_SCALE).astype(BF16)
        q_ref[hd, :, 256:] = (rope(qr[:, sl]) * SCORE_SCALE).astype(BF16)


def _mla_proj(xp, xs, tab, g_mix, w_in, g_q, g_kv, w_qn, w_qr, w_ukT):
    (tp, d), ts = xp.shape, xs.shape[0]
    t = tp + ts
    tm = TOKEN_TILE
    nh = w_ukT.shape[0]
    npt = tp // tm
    const = lambda *shape: pl.BlockSpec(shape, lambda i: (0,) * len(shape))
    row = lambda w: pl.BlockSpec((tm, w), lambda i: (i, 0))
    prow = lambda w: pl.BlockSpec((tm, w), lambda i: (jnp.minimum(i, npt - 1), 0))
    srow = lambda w: pl.BlockSpec((tm, w), lambda i: (jnp.maximum(i - npt, 0), 0))
    seq_tiles = (tab.shape[0] - ts) // tm
    tab_rows = lambda i: (jnp.where(i < npt, i % seq_tiles, seq_tiles + i - npt), 0)
    return pl.pallas_call(
        functools.partial(_mla_proj_kernel, n_prompt_tiles=npt, n_heads=nh),
        grid=(t // tm,),
        in_specs=[prow(d), srow(d), pl.BlockSpec((tm, LANES), tab_rows),
                  const(1, d), const(*w_in.shape), const(1, 384), const(1, 256),
                  const(*w_qn.shape), const(*w_qr.shape), const(*w_ukT.shape)],
        out_specs=[pl.BlockSpec((nh, tm, QK_WIDTH), lambda i: (0, i, 0)), row(QK_WIDTH),
                   prow(256), prow(QK_ROPE), srow(256), srow(QK_ROPE)],
        out_shape=[jax.ShapeDtypeStruct((nh, t, QK_WIDTH), BF16), jax.ShapeDtypeStruct((t, QK_WIDTH), BF16),
                   jax.ShapeDtypeStruct((tp, 256), F32), jax.ShapeDtypeStruct((tp, QK_ROPE), F32),
                   jax.ShapeDtypeStruct((ts, 256), F32), jax.ShapeDtypeStruct((ts, QK_ROPE), F32)],
        compiler_params=_params("arbitrary"),
        name="mla_proj",
    )(xp, xs, tab, g_mix, w_in, g_q, g_kv, w_qn, w_qr, w_ukT)


def _prompt_attn_kernel(q_ref, k_ref, o_ref, s_ref, mx_ref, sum_ref, acc_ref):
    j = pl.program_id(1)
    n_blocks = pl.num_programs(1) - 1
    nh, qb, w = q_ref.shape
    rows = nh * qb
    ck = ATTN_KV_CHUNK
    cur = j % 2
    prev = 1 - cur
    full_cur = (j * qb) // ck
    full_prev = (jnp.maximum(j - 1, 0) * qb) // ck

    def key_chunk(c):
        return k_ref[pl.ds(pl.multiple_of(c * ck, ck), ck), :]

    def scores(c, masked):
        s = _dot_nt(q_ref[...].reshape(rows, w), key_chunk(c))
        if masked:
            qpos = j * qb + (lax.broadcasted_iota(jnp.int32, s.shape, 0) & (qb - 1))
            kpos = c * ck + lax.broadcasted_iota(jnp.int32, s.shape, 1)
            s = jnp.where(kpos <= qpos, s, NEG_INF)
        s_ref[cur, c] = s
        mx_ref[cur] = jnp.maximum(mx_ref[cur], jnp.maximum(s[:, :LANES], s[:, LANES:]))

    def pv(c):
        m = mx_ref[prev]
        p0 = jnp.exp2(s_ref[prev, c, :, :LANES] - m)
        p1 = jnp.exp2(s_ref[prev, c, :, LANES:] - m)
        sum_ref[...] += p0 + p1
        p = jnp.concatenate([p0, p1], axis=1).astype(BF16)
        acc_ref[...] += _dot(p, key_chunk(c)[:, :256])

    def finish_prev():
        o = acc_ref[...] / jnp.sum(sum_ref[...], axis=-1, keepdims=True)
        o_ref[...] = o.reshape(nh, qb, 256).astype(BF16)

    sum_ref[...] = jnp.zeros(sum_ref.shape, F32)
    acc_ref[...] = jnp.zeros(acc_ref.shape, F32)

    @pl.when(j < n_blocks)
    def _():
        mx_ref[cur] = jnp.full(mx_ref.shape[1:], NEG_INF, F32)

        def body(c, carry):
            pv(c)
            scores(c, False)
            return carry
        lax.fori_loop(0, full_cur, body, 0)
        scores(full_cur, True)
        mx_ref[cur] = jnp.broadcast_to(jnp.max(mx_ref[cur], axis=-1, keepdims=True), mx_ref.shape[1:])

        @pl.when(full_prev == full_cur)
        def _():
            @pl.when(j > 0)
            def _():
                pv(full_prev)

        @pl.when(j > 0)
        def _():
            finish_prev()

    @pl.when(j == n_blocks)
    def _():
        def body(c, carry):
            pv(c)
            return carry
        lax.fori_loop(0, full_prev + 1, body, 0)
        finish_prev()


def _prompt_attn(q, k, *, batch, seq):
    nh = q.shape[0]
    qb = ATTN_Q_BLOCK
    nj = seq // qb
    rows = nh * qb
    q_block = lambda b, j: (0, b * nj + jnp.minimum(j, nj - 1), 0)
    o_block = lambda b, j: (0, b * nj + jnp.maximum(j - 1, 0), 0)
    return pl.pallas_call(
        _prompt_attn_kernel,
        grid=(batch, nj + 1),
        in_specs=[pl.BlockSpec((nh, qb, QK_WIDTH), q_block),
                  pl.BlockSpec((seq, QK_WIDTH), lambda b, j: (b, 0))],
        out_specs=pl.BlockSpec((nh, qb, 256), o_block),
        out_shape=jax.ShapeDtypeStruct((nh, batch * seq, 256), BF16),
        scratch_shapes=[pltpu.VMEM((2, seq // ATTN_KV_CHUNK, rows, ATTN_KV_CHUNK), F32),
                        pltpu.VMEM((2, rows, LANES), F32), pltpu.VMEM((rows, LANES), F32),
                        pltpu.VMEM((rows, 256), F32)],
        compiler_params=_params("arbitrary", "arbitrary"),
        name="prompt_attn",
    )(q, k)


def _sample_attn_kernel(pt_ref, q_ref, knew_ref, ckv_hbm, krt_hbm, o_ref, cbuf, rbuf, kb, s_ref, sem,
                        *, n_new, page_base):
    b = pl.program_id(0)
    nb = pl.num_programs(0)
    slot = b % 2
    n_pages, page, lat = cbuf.shape[1:]

    def issue(bb, sl):
        def body(p, c):
            pg = pt_ref[bb, p] + page_base
            pltpu.make_async_copy(ckv_hbm.at[pg], cbuf.at[sl, p], sem.at[0, sl]).start()
            pltpu.make_async_copy(krt_hbm.at[pg], rbuf.at[sl, p], sem.at[1, sl]).start()
            return c
        lax.fori_loop(0, n_pages, body, 0, unroll=4)

    @pl.when(b == 0)
    def _():
        issue(0, 0)

    @pl.when(b + 1 < nb)
    def _():
        issue(b + 1, 1 - slot)

    pltpu.make_async_copy(ckv_hbm.at[pl.ds(0, n_pages)], cbuf.at[slot], sem.at[0, slot]).wait()
    pltpu.make_async_copy(krt_hbm.at[pl.ds(0, n_pages)], rbuf.at[slot], sem.at[1, slot]).wait()

    q = q_ref[0]
    ql, qr = q[:, :lat], q[:, lat:lat + QK_ROPE]
    ppc = 8
    ch = ppc * page
    n_chunks = n_pages // ppc
    for c in range(n_chunks):
        cb = cbuf[slot, c * ppc:(c + 1) * ppc].reshape(ch, lat).astype(BF16)
        kb[c * ch:(c + 1) * ch, :] = cb
        rt = jnp.concatenate([rbuf[slot, c * ppc + i] for i in range(ppc)], axis=1).astype(BF16)
        s_ref[:, c * ch:(c + 1) * ch] = _dot_nt(ql, cb) + _dot(qr, rt)

    knew = knew_ref[0]
    sn = _dot_nt(q[:, :QK_WIDTH], knew)
    qi = lax.broadcasted_iota(jnp.int32, sn.shape, 0) & (n_new - 1)
    ki = lax.broadcasted_iota(jnp.int32, sn.shape, 1)
    sn = jnp.where((ki < n_new) & (ki <= qi), sn, NEG_INF)
    s = s_ref[...]
    m = jnp.maximum(jnp.max(s, axis=-1, keepdims=True), jnp.max(sn, axis=-1, keepdims=True))
    pn = jnp.exp2(sn - m)
    s_ref[...] = jnp.exp2(s - m)
    l = jnp.sum(s_ref[...], axis=-1, keepdims=True) + jnp.sum(pn, axis=-1, keepdims=True)
    o = _dot(pn.astype(BF16), knew[:, :lat])
    for c in range(n_chunks):
        o += _dot(s_ref[:, c * ch:(c + 1) * ch].astype(BF16), kb[c * ch:(c + 1) * ch, :])
    o_ref[0] = (o / l).astype(BF16)


def _sample_attn(page_table, qs, knew, ckv_pages, krt_pages, *, n_new, page_base):
    nb, rows, qw = qs.shape
    n_pages = page_table.shape[1]
    page, lat = ckv_pages.shape[1:]
    rope = krt_pages.shape[1]
    n_keys = n_pages * page
    grid_spec = pltpu.PrefetchScalarGridSpec(
        num_scalar_prefetch=1,
        grid=(nb,),
        in_specs=[pl.BlockSpec((1, rows, qw), lambda b, pt: (b, 0, 0)),
                  pl.BlockSpec((1,) + knew.shape[1:], lambda b, pt: (b, 0, 0)),
                  pl.BlockSpec(memory_space=pl.ANY), pl.BlockSpec(memory_space=pl.ANY)],
        out_specs=pl.BlockSpec((1, rows, lat), lambda b, pt: (b, 0, 0)),
        scratch_shapes=[pltpu.VMEM((2, n_pages, page, lat), F32), pltpu.VMEM((2, n_pages, rope, page), F32),
                        pltpu.VMEM((n_keys, lat), BF16), pltpu.VMEM((rows, n_keys), F32),
                        pltpu.SemaphoreType.DMA((2, 2))],
    )
    return pl.pallas_call(
        functools.partial(_sample_attn_kernel, n_new=n_new, page_base=page_base),
        grid_spec=grid_spec,
        out_shape=jax.ShapeDtypeStruct((nb, rows, lat), BF16),
        compiler_params=_params("arbitrary"),
        name="sample_attn",
    )(page_table, qs, knew, ckv_pages, krt_pages)


def _mla_out_kernel(xp_ref, xs_ref, op_ref, os_ref, wuv_ref, wo_ref, out_ref, *, n_prompt_tiles):
    is_sample = pl.program_id(0) >= n_prompt_tiles
    nh = op_ref.shape[0]
    heads = []
    for hd in range(nh):
        o = jnp.where(is_sample, os_ref[hd], op_ref[hd])
        heads.append(_dot(o, wuv_ref[hd]).astype(BF16))
    x = jnp.where(is_sample, xs_ref[...], xp_ref[...])
    out_ref[...] = x + _dot(jnp.concatenate(heads, axis=-1), wo_ref[...])


def _mla_out(xp, xs, o_p, o_s, w_uv, w_o):
    (tp, d), ts = xp.shape, xs.shape[0]
    t = tp + ts
    tm = TOKEN_TILE
    npt = tp // tm
    nh = w_uv.shape[0]
    return pl.pallas_call(
        functools.partial(_mla_out_kernel, n_prompt_tiles=npt),
        grid=(t // tm,),
        in_specs=[pl.BlockSpec((tm, d), lambda i: (jnp.minimum(i, npt - 1), 0)),
                  pl.BlockSpec((tm, d), lambda i: (jnp.maximum(i - npt, 0), 0)),
                  pl.BlockSpec((nh, tm, 256), lambda i: (0, jnp.minimum(i, npt - 1), 0)),
                  pl.BlockSpec((nh, tm, 256), lambda i: (0, jnp.maximum(i - npt, 0), 0)),
                  pl.BlockSpec(w_uv.shape, lambda i: (0, 0, 0)),
                  pl.BlockSpec(w_o.shape, lambda i: (0, 0))],
        out_specs=pl.BlockSpec((tm, d), lambda i: (i, 0)),
        out_shape=jax.ShapeDtypeStruct((t, d), F32),
        compiler_params=_params("arbitrary"),
        name="mla_out",
    )(xp, xs, o_p, o_s, w_uv, w_o)


def _ffn_pre_kernel(h_ref, g_ref, wr_ref, br_ref, text_ref, cnt_ref, carry_ref):
    i = pl.program_id(0)
    tm, d = h_ref.shape
    n_exp = N_EXPERT_GROUPS * EXPERTS_PER_GROUP

    @pl.when(i == 0)
    def _():
        carry_ref[...] = jnp.zeros(carry_ref.shape, F32)

    t = _rms(h_ref[...], g_ref[...])
    text_ref[:, :d] = t
    logits = jnp.dot(t, wr_ref[...], precision=lax.Precision.HIGHEST, preferred_element_type=F32) + br_ref[...]
    lane_i = lax.broadcasted_iota(jnp.int32, (tm, LANES), 1)
    lane = lane_i.astype(F32)
    lane_group = (lane_i >> (EXPERTS_PER_GROUP.bit_length() - 1)).astype(F32)

    def first_lane_of_max(vals, valid):
        top = jnp.max(jnp.where(valid, vals, -jnp.inf), axis=-1, keepdims=True)
        idx = jnp.min(jnp.where(valid & (vals == top), lane, float(LANES)), axis=-1, keepdims=True)
        return top, idx

    def masked_softmax(valid):
        mx = jnp.max(jnp.where(valid, logits, -jnp.inf), axis=-1, keepdims=True)
        e = jnp.where(valid, jnp.exp(logits - mx), 0.0)
        return e / jnp.sum(e, axis=-1, keepdims=True)

    is_group = (lane >= n_exp) & (lane < n_exp + N_EXPERT_GROUPS)
    pg = masked_softmax(is_group)
    pg_top, g_lane = first_lane_of_max(pg, is_group)
    g_idx = g_lane - n_exp
    in_group = (lane < n_exp) & (lane_group == g_idx)
    pe = masked_softmax(in_group)
    p1, i1 = first_lane_of_max(pe, in_group)
    rest = in_group & (lane != i1)
    p2, i2 = first_lane_of_max(pe, rest)
    denom = p1 + p2
    gates = jnp.where(lane == i1, pg_top * p1 / denom, 0.0) + jnp.where(lane == i2, pg_top * p2 / denom, 0.0)

    onehot = (lane == g_idx).astype(F32)
    rr = lax.broadcasted_iota(jnp.int32, (tm, tm), 0)
    cc = lax.broadcasted_iota(jnp.int32, (tm, tm), 1)
    strict_lower = (cc < rr).astype(BF16)
    before = _dot(strict_lower, onehot.astype(BF16)) + carry_ref[0:1, :]
    rank = jnp.sum(onehot * before, axis=-1, keepdims=True)
    carry_ref[...] = carry_ref[...] + jnp.sum(onehot, axis=0, keepdims=True)
    cnt_ref[...] = carry_ref[...]

    route = jnp.where(lane == ROUTE_GIDX_LANE, g_idx, gates)
    route = jnp.where(lane == ROUTE_RANK_LANE, rank, route)
    text_ref[:, d:] = route


def _ffn_pre(h, g_ffn, w_r, b_r):
    t, d = h.shape
    tm = TOKEN_TILE
    return pl.pallas_call(
        _ffn_pre_kernel,
        grid=(t // tm,),
        in_specs=[pl.BlockSpec((tm, d), lambda i: (i, 0)), pl.BlockSpec((1, d), lambda i: (0, 0)),
                  pl.BlockSpec(w_r.shape, lambda i: (0, 0)), pl.BlockSpec((1, LANES), lambda i: (0, 0))],
        out_specs=[pl.BlockSpec((tm, d + LANES), lambda i: (i, 0)), pl.BlockSpec((8, LANES), lambda i: (0, 0))],
        out_shape=[jax.ShapeDtypeStruct((t, d + LANES), F32), jax.ShapeDtypeStruct((8, LANES), F32)],
        scratch_shapes=[pltpu.VMEM((8, LANES), F32)],
        compiler_params=_params("arbitrary"),
        name="ffn_pre",
    )(h, g_ffn, w_r, b_r)


def _moe_ffn_kernel(tg_ref, src_cur_ref, src_next_ref, dst_prev_ref, dst_cur_ref, text_hbm,
                    wg_ref, wu_ref, wd_ref, ytok_hbm, xbuf, ybuf, sem):
    k = pl.program_id(0)
    nk = pl.num_programs(0)
    g = tg_ref[k]
    slot = k % 2
    other = 1 - slot
    tm, d = ybuf.shape[1], ybuf.shape[2]
    per_expert = tm // EXPERTS_PER_GROUP

    def gather_start(tok, sl, r):
        pltpu.make_async_copy(text_hbm.at[pl.ds(tok, 1)], xbuf.at[sl, pl.ds(r, 1)], sem.at[0, sl]).start()

    def scatter_start(tok, sl, r):
        pltpu.make_async_copy(ybuf.at[sl, pl.ds(r, 1)], ytok_hbm.at[pl.ds(tok, 1)], sem.at[1, sl]).start()

    def gather_wait(sl):
        pltpu.make_async_copy(text_hbm.at[pl.ds(0, tm)], xbuf.at[sl], sem.at[0, sl]).wait()

    def scatter_wait(sl):
        pltpu.make_async_copy(ybuf.at[sl], ytok_hbm.at[pl.ds(0, tm)], sem.at[1, sl]).wait()

    @pl.when(k == 0)
    def _():
        ybuf[...] = jnp.zeros(ybuf.shape, F32)

        def body(r, c):
            gather_start(src_cur_ref[0, 0, r], 0, r)
            return c
        lax.fori_loop(0, tm, body, 0, unroll=8)

    gather_wait(slot)
    x = xbuf[slot, :, :d].astype(BF16)
    route = xbuf[slot, :, d:]
    lane = lax.broadcasted_iota(jnp.int32, route.shape, 1)
    hid = wg_ref.shape[3]
    for e0 in range(0, EXPERTS_PER_GROUP, MOE_EXPERT_BLOCK):
        for r in range(e0 * per_expert, (e0 + MOE_EXPERT_BLOCK) * per_expert):
            gather_start(src_next_ref[0, 0, r], other, r)
            scatter_start(dst_prev_ref[0, 0, r], other, r)
        acts = []
        for e in range(e0, e0 + MOE_EXPERT_BLOCK):
            gate = jnp.sum(jnp.where(lane == g * EXPERTS_PER_GROUP + e, route, 0.0), axis=-1, keepdims=True)
            a1 = _dot(x, wg_ref[0, e])
            a2 = _dot(x, wu_ref[0, e])
            acts.append((a1 * (1.0 / (1.0 + jnp.exp(-a1))) * a2 * gate).astype(BF16))
        ye = _dot(jnp.concatenate(acts, axis=-1), wd_ref[0, e0 * hid:(e0 + MOE_EXPERT_BLOCK) * hid, :])
        if e0 == 0:
            ybuf[slot] = ye
        else:
            ybuf[slot] += ye
    scatter_wait(other)

    @pl.when(k == nk - 1)
    def _():
        def body(r, c):
            scatter_start(dst_cur_ref[0, 0, r], slot, r)
            return c
        lax.fori_loop(0, tm, body, 0, unroll=8)
        scatter_wait(slot)
        gather_wait(other)


def _moe_ffn(tile_group, src, dst, t_ext, w_gate, w_up, w_down, *, d, n_dump, first_group):
    t = t_ext.shape[0]
    n_tiles, _, tm = src.shape
    src_next = jnp.concatenate([src[1:], jnp.zeros_like(src[:1])], axis=0)
    dst_prev = jnp.concatenate([(t + jnp.arange(tm, dtype=jnp.int32)).reshape(1, 1, tm), dst[:-1]], axis=0)
    smem = lambda: pl.BlockSpec((1, 1, tm), lambda k, tg: (k, 0, 0), memory_space=pltpu.SMEM)
    grid_spec = pltpu.PrefetchScalarGridSpec(
        num_scalar_prefetch=1,
        grid=(n_tiles,),
        in_specs=[smem(), smem(), smem(), smem(),
                  pl.BlockSpec(memory_space=pl.ANY),
                  pl.BlockSpec((1,) + w_gate.shape[1:], lambda k, tg: (first_group + tg[k], 0, 0, 0)),
                  pl.BlockSpec((1,) + w_up.shape[1:], lambda k, tg: (first_group + tg[k], 0, 0, 0)),
                  pl.BlockSpec((1,) + w_down.shape[1:], lambda k, tg: (first_group + tg[k], 0, 0))],
        out_specs=pl.BlockSpec(memory_space=pl.ANY),
        scratch_shapes=[pltpu.VMEM((2, tm, t_ext.shape[1]), F32), pltpu.VMEM((2, tm, d), F32),
                        pltpu.SemaphoreType.DMA((2, 2))],
    )
    return pl.pallas_call(
        _moe_ffn_kernel,
        grid_spec=grid_spec,
        out_shape=jax.ShapeDtypeStruct((t + n_dump, d), F32),
        compiler_params=_params("arbitrary"),
        name="moe_ffn",
    )(tile_group, src, src_next, dst_prev, dst, t_ext, w_gate, w_up, w_down)


def _hier_moe(h, g_ffn, w_rg, b_rg, w_re, b_re, wg, wu, wd, *, layer):
    t, d = h.shape
    tm = TOKEN_TILE
    ng, epg = N_EXPERT_GROUPS, EXPERTS_PER_GROUP
    n_exp = ng * epg
    pad = LANES - n_exp - ng
    w_r = jnp.concatenate([w_re, w_rg, jnp.zeros((d, pad), F32)], axis=1)
    b_r = jnp.concatenate([b_re, b_rg, jnp.zeros((pad,), F32)])[None, :]
    t_ext, cnt = _ffn_pre(h, g_ffn[None, :], w_r, b_r)

    g_idx = t_ext[:, d + ROUTE_GIDX_LANE].astype(jnp.int32)
    rank = t_ext[:, d + ROUTE_RANK_LANE].astype(jnp.int32)
    counts = cnt[0, :ng].astype(jnp.int32)
    tiles_per_group = (counts + tm - 1) // tm
    tile_end = jnp.cumsum(tiles_per_group)
    tile_start = tile_end - tiles_per_group
    n_tiles = t // tm + ng - 1
    pos = tile_start[g_idx] * tm + rank
    slot_id = jnp.arange(n_tiles * tm, dtype=jnp.int32)
    dst = (t + slot_id % tm).at[pos].set(jnp.arange(t, dtype=jnp.int32), unique_indices=True)
    src = jnp.where(dst < t, dst, 0)
    tile_id = jnp.arange(n_tiles, dtype=jnp.int32)
    tile_group = jnp.minimum(jnp.sum(tile_id[:, None] >= tile_end[None, :], axis=1), ng - 1).astype(jnp.int32)
    return _moe_ffn(tile_group, src.reshape(n_tiles, 1, tm), dst.reshape(n_tiles, 1, tm), t_ext, wg, wu, wd,
                    d=d, n_dump=tm, first_group=layer * ng)


def _ple_update(h_ref, y_ref, pp_ref, ps_ref, g_ref, wgate_ref, wproj_ref, is_sample):
    h = h_ref[...] + y_ref[...]
    gate_in = _dot(_rms(h, g_ref[...]).astype(BF16), wgate_ref[...])
    gate = 1.0 / (1.0 + jnp.exp(-gate_in))
    p = jnp.where(is_sample, ps_ref[...], pp_ref[...]).astype(BF16)
    return h + gate * _dot(p, wproj_ref[...])


def _ple_kernel(h_ref, y_ref, pp_ref, ps_ref, g_ref, wgate_ref, wproj_ref, out_ref, *, n_prompt_tiles):
    is_sample = pl.program_id(0) >= n_prompt_tiles
    out_ref[...] = _ple_update(h_ref, y_ref, pp_ref, ps_ref, g_ref, wgate_ref, wproj_ref, is_sample)


def _ple_final_kernel(h_ref, y_ref, pp_ref, ps_ref, g_ref, wgate_ref, wproj_ref, gfin_ref, outp_ref, outs_ref,
                      *, n_prompt_tiles):
    is_sample = pl.program_id(0) >= n_prompt_tiles
    y = _rms(_ple_update(h_ref, y_ref, pp_ref, ps_ref, g_ref, wgate_ref, wproj_ref, is_sample), gfin_ref[...])

    @pl.when(jnp.logical_not(is_sample))
    def _():
        outp_ref[...] = y

    @pl.when(is_sample)
    def _():
        outs_ref[...] = y


def _ple(h, y, p_prompt, p_sample, g_ple, w_gate, w_proj, g_final=None, *, layer, tp, ts):
    t, d = h.shape
    tm = TOKEN_TILE
    npt = tp // tm
    pdim = p_prompt.shape[2]
    row = lambda w: pl.BlockSpec((tm, w), lambda i: (i, 0))
    const = lambda *shape: pl.BlockSpec(shape, lambda i: (0,) * len(shape))
    prompt_rows = lambda i: jnp.minimum(i, npt - 1)
    sample_rows = lambda i: jnp.maximum(i - npt, 0)
    in_specs = [row(d), row(d),
                pl.BlockSpec((None, tm, pdim), lambda i: (layer, prompt_rows(i), 0)),
                pl.BlockSpec((None, tm, pdim), lambda i: (layer, sample_rows(i), 0)),
                const(1, d), const(*w_gate.shape), const(*w_proj.shape)]
    if g_final is None:
        return pl.pallas_call(
            functools.partial(_ple_kernel, n_prompt_tiles=npt), grid=(t // tm,), in_specs=in_specs,
            out_specs=row(d), out_shape=jax.ShapeDtypeStruct((t, d), F32),
            compiler_params=_params("arbitrary"), name="ple",
        )(h, y, p_prompt, p_sample, g_ple, w_gate, w_proj)
    return pl.pallas_call(
        functools.partial(_ple_final_kernel, n_prompt_tiles=npt),
        grid=(t // tm,),
        in_specs=in_specs + [const(1, d)],
        out_specs=[pl.BlockSpec((tm, d), lambda i: (prompt_rows(i), 0)),
                   pl.BlockSpec((tm, d), lambda i: (sample_rows(i), 0))],
        out_shape=[jax.ShapeDtypeStruct((tp, d), F32), jax.ShapeDtypeStruct((ts, d), F32)],
        compiler_params=_params("arbitrary"),
        name="ple_final",
    )(h, y, p_prompt, p_sample, g_ple, w_gate, w_proj, g_final)


def _gelu(x):
    c0 = -2.0 * LOG2_E * 0.7978845608028654
    c1 = c0 * 0.044715
    return x / (1.0 + jnp.exp2((c0 + c1 * (x * x)) * x))


def _sgu_kernel(h_ref, g_ref, win_ref, bin_ref, gv_ref, bv_ref, ws_ref, bs_ref, wout_ref,
                out_ref, vs_ref, vn_ref, u_ref, us_ref, *, n_prompt_tiles, n_new):
    i = pl.program_id(0)
    tm = h_ref.shape[0]
    half = gv_ref.shape[1]
    gd = half // SGU_GROUPS
    ck = SGU_CHUNK
    is_sample = i >= n_prompt_tiles
    x = h_ref[...]
    hn = _rms(x, g_ref[...]).astype(BF16)
    u_ref[...] = _gelu(_dot(hn, win_ref[:, :half]) + bin_ref[:, :half])
    v = _gelu(_dot(hn, win_ref[:, half:]) + bin_ref[:, half:])
    mu = jnp.mean(v, axis=-1, keepdims=True)
    vc = v - mu
    var = jnp.mean(vc * vc, axis=-1, keepdims=True)
    vn = vc * lax.rsqrt(var + LN_EPS) * gv_ref[...] + bv_ref[...]
    vn_ref[...] = vn

    @pl.when(is_sample)
    def _():
        vs_ref[...] = vn

    rr = lax.broadcasted_iota(jnp.int32, (ck, ck), 0)
    cc = lax.broadcasted_iota(jnp.int32, (ck, ck), 1)
    shift = n_new.bit_length() - 1
    same_seq = (rr >> shift) == (cc >> shift)
    mask = (cc <= rr) & (same_seq | jnp.logical_not(is_sample))
    for g in range(SGU_GROUPS):
        cols = slice(g * gd, (g + 1) * gd)
        wm = jnp.where(mask, ws_ref[0, g], 0.0).astype(BF16)
        bias = bs_ref[0, :, g:g + 1]
        for c in range(tm // ck):
            rows = slice(c * ck, (c + 1) * ck)
            s = _dot(wm, vn_ref[rows, cols].astype(BF16)) + bias
            us_ref[rows, cols] = (u_ref[rows, cols] * s).astype(BF16)
    out_ref[...] = x + _dot(us_ref[...], wout_ref[...])


def _sgu(h, g_mix, w_in, b_in, g_v, b_v, w_s2, b_s2, w_out, *, tp, ts, n_new):
    t, d = h.shape
    tm = SGU_TILE
    npt = tp // tm
    half = g_v.shape[1]
    const = lambda *shape: pl.BlockSpec(shape, lambda i: (0,) * len(shape))
    resident = lambda *shape: pl.BlockSpec(shape, lambda i: (0,) * len(shape), pipeline_mode=pl.Buffered(1))
    variant = lambda i: jnp.minimum(i // npt, 1)
    return pl.pallas_call(
        functools.partial(_sgu_kernel, n_prompt_tiles=npt, n_new=n_new),
        grid=(t // tm,),
        in_specs=[pl.BlockSpec((tm, d), lambda i: (i, 0)), const(1, d), resident(*w_in.shape), const(1, 2 * half),
                  const(1, half), const(1, half),
                  pl.BlockSpec((1,) + w_s2.shape[1:], lambda i: (variant(i), 0, 0, 0)),
                  pl.BlockSpec((1,) + b_s2.shape[1:], lambda i: (variant(i), 0, 0)),
                  resident(*w_out.shape)],
        out_specs=[pl.BlockSpec((tm, d), lambda i: (i, 0)),
                   pl.BlockSpec((tm, half), lambda i: (jnp.maximum(i - npt, 0), 0))],
        out_shape=[jax.ShapeDtypeStruct((t, d), F32), jax.ShapeDtypeStruct((ts, half), F32)],
        scratch_shapes=[pltpu.VMEM((tm, half), F32), pltpu.VMEM((tm, half), F32), pltpu.VMEM((tm, half), BF16)],
        compiler_params=_params("arbitrary"),
        name="sgu",
    )(h, g_mix, w_in, b_in, g_v, b_v, w_s2, b_s2, w_out)


def kernel(x_prompt, x_sample, cache_ckv, cache_krope, page_table, p_prompt, p_sample, g_mix, g_ffn, g_ple, g_final, mla_w_in, mla_g_q, mla_g_kv, mla_w_qb, mla_w_uk, mla_w_uv, mla_w_o, sgu_w_in, sgu_b_in, sgu_g_v, sgu_b_v, sgu_w_s, sgu_b_s, sgu_w_out, moe_w_rg, moe_b_rg, moe_w_re, moe_b_re, moe_w_gate, moe_w_up, moe_w_down, ple_w_proj, ple_w_gate):
    bp, seq, d = x_prompt.shape
    bs, n_new, _ = x_sample.shape
    tp, ts = bp * seq, bs * n_new
    t = tp + ts
    depth = g_mix.shape[0]
    assert depth == 2 and seq % TOKEN_TILE == 0 and ts % TOKEN_TILE == 0 and n_new & (n_new - 1) == 0 and n_new <= 16
    xp, xs = x_prompt.reshape(tp, d), x_sample.reshape(ts, d)
    pp, ps = p_prompt.reshape(depth, tp, -1), p_sample.reshape(depth, ts, -1)

    half = QK_ROPE // 2
    swap = lambda w: jnp.concatenate([w[..., half:], w[..., :half]], axis=-1)
    inv_freq = 1.0 / (ROPE_THETA ** (jnp.arange(0, QK_ROPE, 2, dtype=F32) / QK_ROPE))
    pos = jnp.concatenate([jnp.arange(seq, dtype=jnp.int32),
                           jnp.tile(PAST_LEN + jnp.arange(n_new, dtype=jnp.int32), bs)])
    ang = pos.astype(F32)[:, None] * inv_freq[None, :]
    cos, sin = jnp.cos(ang), jnp.sin(ang)
    tab = jnp.concatenate([cos, cos, -sin, sin], axis=1)

    w_in = mla_w_in[0]
    n_lat = w_in.shape[1] - QK_ROPE
    w_in_ext = jnp.concatenate([w_in, swap(w_in[:, n_lat:])], axis=1).astype(BF16)
    nh = mla_w_uk.shape[2]
    w_qb = mla_w_qb[0].reshape(-1, nh, QK_NOPE + QK_ROPE)
    q_lora = w_qb.shape[0]
    w_qn = w_qb[:, :, :QK_NOPE].reshape(q_lora, nh * QK_NOPE).astype(BF16)
    w_qr = jnp.concatenate([w_qb[:, :, QK_NOPE:], swap(w_qb[:, :, QK_NOPE:])], axis=-1)
    w_qr = w_qr.reshape(q_lora, nh * 2 * QK_ROPE).astype(BF16)
    w_ukT = mla_w_uk[0].transpose(1, 2, 0).astype(BF16)
    w_uv = mla_w_uv[0].transpose(1, 0, 2).astype(BF16)
    w_o = mla_w_o[0].astype(BF16)

    q_all, k_all, ckv_p, kr_p, ckv_s, kr_s = _mla_proj(
        xp, xs, tab, g_mix[0][None, :], w_in_ext, mla_g_q[0][None, :], mla_g_kv[0][None, :], w_qn, w_qr, w_ukT)
    o_p = _prompt_attn(q_all, k_all, batch=bp, seq=seq)

    lat = ckv_p.shape[1]
    rows = nh * n_new
    qs = q_all[:, tp:, :].reshape(nh, bs, n_new, QK_WIDTH).transpose(1, 0, 2, 3).reshape(bs, rows, QK_WIDTH)
    knew = jnp.pad(k_all[tp:].reshape(bs, n_new, QK_WIDTH), ((0, 0), (0, 16 - n_new), (0, 0)))
    page = cache_ckv.shape[2]
    attn_layer = 0
    ckv_pages = cache_ckv.reshape(-1, page, lat)
    krt_pages = cache_krope.transpose(0, 1, 3, 2).reshape(-1, QK_ROPE, page)
    o_s = _sample_attn(page_table, qs, knew, ckv_pages, krt_pages, n_new=n_new,
                       page_base=attn_layer * cache_ckv.shape[1])
    o_s = o_s.reshape(bs, nh, n_new, lat).transpose(1, 0, 2, 3).reshape(nh, ts, lat)
    h = _mla_out(xp, xs, o_p, o_s, w_uv, w_o)

    ng, epg = N_EXPERT_GROUPS, EXPERTS_PER_GROUP
    hid = moe_w_gate.shape[-1]
    wg_all = moe_w_gate.astype(BF16).reshape(depth * ng, epg, d, hid)
    wu_all = moe_w_up.astype(BF16).reshape(depth * ng, epg, d, hid)
    wd_all = moe_w_down.astype(BF16).reshape(depth * ng, epg * hid, d)

    def ffn_and_ple(h, i, final):
        y = _hier_moe(h, g_ffn[i], moe_w_rg[i], moe_b_rg[i], moe_w_re[i], moe_b_re[i],
                      wg_all, wu_all, wd_all, layer=i)
        args = (h, y, pp, ps, g_ple[i][None, :], ple_w_gate[i].astype(BF16), ple_w_proj[i].astype(BF16))
        return _ple(*args, g_final[None, :] if final else None, layer=i, tp=tp, ts=ts)

    h = ffn_and_ple(h, 0, False)

    w_s = sgu_w_s[0]
    reps = SGU_CHUNK // n_new
    w_s2 = jnp.stack([w_s, jnp.tile(w_s[:, :n_new, :n_new], (1, reps, reps))])
    b_s = sgu_b_s[0]
    b_s2 = jnp.stack([b_s.T, jnp.tile(b_s[:, :n_new], (1, reps)).T])
    h, v_s = _sgu(h, g_mix[1][None, :], sgu_w_in[0].astype(BF16), sgu_b_in[0][None, :], sgu_g_v[0][None, :],
                  sgu_b_v[0][None, :], w_s2, b_s2, sgu_w_out[0].astype(BF16), tp=tp, ts=ts, n_new=n_new)
    y_p, y_s = ffn_and_ple(h, 1, True)

    return (y_p.reshape(bp, seq, d), y_s.reshape(bs, n_new, d),
            ckv_p.reshape(1, bp, seq, lat), kr_p.reshape(1, bp, seq, QK_ROPE),
            ckv_s.reshape(1, bs, n_new, lat), kr_s.reshape(1, bs, n_new, QK_ROPE),
            v_s.reshape(1, bs, n_new, -1))
```
